```python
import math
import jax, jax.numpy as jnp
from jax import lax
import numpy as np

D_MODEL = 2048
BATCH = 4
SEQ = 2048
DEPTH = 4
DEC_BATCH = 8
DEC_SEQ = 8
PAST_LEN = 16384
PAGE_SIZE = 128

N_EVEN = (DEPTH + 1) // 2
N_ODD = DEPTH // 2
N_SUB = 3
MACARON = 0.5
D_FF = 5632
EPS = 1e-6
D_SSD = D_MODEL // 2 * 2 // 2 * 1
D_SSD = 2048
SSD_HEAD_DIM = 64
SSD_HEADS = D_SSD // SSD_HEAD_DIM
SSD_GROUPS = 4
SSD_STATE = 128
SSD_CONV = 4
SSD_CHUNK = 128
SSD_CONV_DIM = D_SSD + 2 * SSD_GROUPS * SSD_STATE
D_CONF = 2048
CONF_KERNEL = 31
IN_EVEN = D_SSD + SSD_CONV_DIM + SSD_HEADS + 2 * D_CONF
ATTN_HEADS = 16
ATTN_HEAD_DIM = D_MODEL // ATTN_HEADS
MOBA_BLOCK = 256
MOBA_TOPK = 3
MOBA_QCHUNK = 16

kernel_name = 'hybrid_ssd_conformer_moba_macaron_adaln_step'


def _rms_norm(x, g):
    xf = x.astype(jnp.float32)
    y = xf * lax.rsqrt(jnp.mean(xf * xf, axis=-1, keepdims=True) + EPS)
    return (y * g.astype(jnp.float32)).astype(x.dtype)


def _group_rms_norm(x, g, groups):
    shp = x.shape
    xf = x.astype(jnp.float32).reshape(shp[:-1] + (groups, shp[-1] // groups))
    y = xf * lax.rsqrt(jnp.mean(xf * xf, axis=-1, keepdims=True) + EPS)
    return (y.reshape(shp) * g.astype(jnp.float32)).astype(x.dtype)


def _layer_norm(x, g, b):
    xf = x.astype(jnp.float32)
    mu = jnp.mean(xf, axis=-1, keepdims=True)
    xc = xf - mu
    y = xc * lax.rsqrt(jnp.mean(xc * xc, axis=-1, keepdims=True) + EPS)
    return (y * g.astype(jnp.float32) + b.astype(jnp.float32)).astype(x.dtype)


def _adaln(x, g, shift, scale):
    return _rms_norm(x, g) * (1 + scale[:, None, :]) + shift[:, None, :]


def _swiglu(h, wg, wu, wd):
    return (jax.nn.silu(h @ wg) * (h @ wu)) @ wd


def _causal_dwconv(x_full, w, bias):
    ch = x_full.shape[-1]
    out = lax.conv_general_dilated(x_full, w[:, None, :].astype(x_full.dtype), window_strides=(1,), padding='VALID', dimension_numbers=('NWC', 'WIO', 'NWC'), feature_group_count=ch)
    return out + bias.astype(x_full.dtype)


def _pad_seq(a, pad):
    return jnp.pad(a, [(0, 0), (0, pad)] + [(0, 0)] * (a.ndim - 2))


def _ssd_scan(x, dt, A, B, C, h0):
    f32 = jnp.float32
    b, L, H, P = x.shape
    G, N = B.shape[2], B.shape[3]
    R = H // G
    Q = min(SSD_CHUNK, L)
    Lp = -(-L // Q) * Q
    pad = Lp - L
    x, B, C, dt = x.astype(f32), B.astype(f32), C.astype(f32), dt.astype(f32)
    if pad:
        x, B, C, dt = _pad_seq(x, pad), _pad_seq(B, pad), _pad_seq(C, pad), _pad_seq(dt, pad)
    nc = Lp // Q
    x = x.reshape(b, nc, Q, G, R, P)
    dt = dt.reshape(b, nc, Q, G, R)
    B = B.reshape(b, nc, Q, G, N)
    C = C.reshape(b, nc, Q, G, N)
    a_cum = jnp.cumsum(dt * A.astype(f32).reshape(G, R), axis=2)
    xdt = x * dt[..., None]
    a_t = jnp.moveaxis(a_cum, 2, -1)
    seg = a_t[..., :, None] - a_t[..., None, :]
    causal = jnp.tril(jnp.ones((Q, Q), dtype=bool))
    decay = jnp.exp(jnp.where(causal, seg, -jnp.inf))
    cb = jnp.einsum('bcqgn,bcsgn->bcgqs', C, B)
    y = jnp.einsum('bcgrqs,bcsgrp->bcqgrp', cb[:, :, :, None] * decay, xdt)
    to_end = jnp.exp(a_cum[:, :, -1:] - a_cum)
    s_chunk = jnp.einsum('bcqgn,bcqgrp->bcgrpn', B, xdt * to_end[..., None])
    chunk_decay = jnp.exp(a_cum[:, :, -1])

    def step(hc, inp):
        dec, s = inp
        return dec[..., None, None] * hc + s, hc

    h_last, h_in = lax.scan(step, h0.astype(f32).reshape(b, G, R, P, N), (jnp.moveaxis(chunk_decay, 1, 0), jnp.moveaxis(s_chunk, 1, 0)))
    h_in = jnp.moveaxis(h_in, 0, 1)
    y = y + jnp.einsum('bcqgn,bcgrpn->bcqgrp', C, h_in) * jnp.exp(a_cum)[..., None]
    y = y.reshape(b, Lp, H, P)[:, :L]
    return y, h_last.reshape(b, H, P, N)


def _ssd_conformer_mixer(h, w_in, w_conv, b_conv, dt_bias, a_log, d_skip, g_ssd, w_dw, b_dw, ln_g, ln_b, w_out, h0, sc_buf, dw_buf):
    b, L, _ = h.shape
    proj = h @ w_in
    i1 = D_SSD
    i2 = i1 + SSD_CONV_DIM
    i3 = i2 + SSD_HEADS
    z, xbc, dt_raw, glu = proj[..., :i1], proj[..., i1:i2], proj[..., i2:i3], proj[..., i3:]
    xbc_full = jnp.concatenate([sc_buf.astype(xbc.dtype), xbc], axis=1)
    new_sc = xbc_full[:, xbc_full.shape[1] - (SSD_CONV - 1):]
    xbc = jax.nn.silu(_causal_dwconv(xbc_full, w_conv, b_conv))
    xs = xbc[..., :D_SSD].reshape(b, L, SSD_HEADS, SSD_HEAD_DIM)
    Bm = xbc[..., D_SSD:D_SSD + SSD_GROUPS * SSD_STATE].reshape(b, L, SSD_GROUPS, SSD_STATE)
    Cm = xbc[..., D_SSD + SSD_GROUPS * SSD_STATE:].reshape(b, L, SSD_GROUPS, SSD_STATE)
    dt = jax.nn.softplus((dt_raw + dt_bias).astype(jnp.float32))
    A = -jnp.exp(a_log.astype(jnp.float32))
    y, h_last = _ssd_scan(xs, dt, A, Bm, Cm, h0)
    y = (y + d_skip.astype(jnp.float32)[:, None] * xs.astype(jnp.float32)).astype(h.dtype)
    y = _group_rms_norm(y.reshape(b, L, D_SSD) * jax.nn.silu(z), g_ssd, SSD_GROUPS)
    u = glu[..., :D_CONF] * jax.nn.sigmoid(glu[..., D_CONF:])
    u_full = jnp.concatenate([dw_buf.astype(u.dtype), u], axis=1)
    new_dw = u_full[:, u_full.shape[1] - (CONF_KERNEL - 1):]
    u = jax.nn.silu(_layer_norm(_causal_dwconv(u_full, w_dw, b_dw), ln_g, ln_b))
    out = jnp.concatenate([y, u], axis=-1) @ w_out
    return out, h_last.astype(h.dtype), new_sc, new_dw


def _moba_attention(q, k_all, v_all, q_start):
    f32 = jnp.float32
    b, Lq, H, Dh = q.shape
    Lk = k_all.shape[1]
    nb = -(-Lk // MOBA_BLOCK)
    pad = nb * MOBA_BLOCK - Lk
    kb = _pad_seq(k_all, pad).reshape(b, nb, MOBA_BLOCK, H, Dh).transpose(0, 3, 1, 2, 4)
    vb = _pad_seq(v_all, pad).reshape(b, nb, MOBA_BLOCK, H, Dh).transpose(0, 3, 1, 2, 4)
    k_mean = jnp.mean(kb.astype(f32), axis=3)
    qc = math.gcd(Lq, MOBA_QCHUNK)
    nq = Lq // qc
    k_sel = min(MOBA_TOPK, nb)
    scale = Dh ** -0.5
    b_idx = jnp.arange(b)[:, None, None, None]
    h_idx = jnp.arange(H)[None, :, None, None]
    blk_ids = jnp.arange(nb)
    kpos_in = jnp.arange(MOBA_BLOCK)

    def chunk(args):
        i, qi = args
        start = q_start + i * qc
        own = start // MOBA_BLOCK
        qpos = start + jnp.arange(qc)
        blk_score = jnp.einsum('bqhd,bhnd->bhqn', qi.astype(f32), k_mean)
        blk_score = jnp.where(blk_ids < own, blk_score, -jnp.inf)
        _, sel = lax.top_k(blk_score, k_sel)
        sel_ok = jnp.arange(k_sel) < own
        kg = kb[b_idx, h_idx, sel]
        vg = vb[b_idx, h_idx, sel]
        ko = lax.dynamic_index_in_dim(kb, own, axis=2, keepdims=False)
        vo = lax.dynamic_index_in_dim(vb, own, axis=2, keepdims=False)
        kpos = own * MOBA_BLOCK + kpos_in
        s_past = jnp.einsum('bqhd,bhqjkd->bhqjk', qi, kg).astype(f32) * scale
        s_past = jnp.where(sel_ok[:, None], s_past, -jnp.inf).reshape(b, H, qc, k_sel * MOBA_BLOCK)
        s_own = jnp.einsum('bqhd,bhkd->bhqk', qi, ko).astype(f32) * scale
        s_own = jnp.where(kpos[None, :] <= qpos[:, None], s_own, -jnp.inf)
        p = jax.nn.softmax(jnp.concatenate([s_past, s_own], axis=-1), axis=-1).astype(qi.dtype)
        p_past = p[..., :k_sel * MOBA_BLOCK].reshape(b, H, qc, k_sel, MOBA_BLOCK)
        p_own = p[..., k_sel * MOBA_BLOCK:]
        return jnp.einsum('bhqjk,bhqjkd->bqhd', p_past, vg) + jnp.einsum('bhqk,bhkd->bqhd', p_own, vo)

    qs = q.reshape(b, nq, qc, H, Dh).transpose(1, 0, 2, 3, 4)
    out = lax.map(chunk, (jnp.arange(nq), qs))
    return out.transpose(1, 0, 2, 3, 4).reshape(b, Lq, H, Dh)


def _moba_mixer(h, w_qkv, w_o, past_k, past_v, q_start):
    b, L, _ = h.shape
    qkv = h @ w_qkv
    q = qkv[..., :D_MODEL].reshape(b, L, ATTN_HEADS, ATTN_HEAD_DIM)
    k = qkv[..., D_MODEL:2 * D_MODEL].reshape(b, L, ATTN_HEADS, ATTN_HEAD_DIM)
    v = qkv[..., 2 * D_MODEL:].reshape(b, L, ATTN_HEADS, ATTN_HEAD_DIM)
    if past_k is None:
        k_all, v_all = k, v
    else:
        k_all = jnp.concatenate([past_k.astype(k.dtype), k], axis=1)
        v_all = jnp.concatenate([past_v.astype(v.dtype), v], axis=1)
    o = _moba_attention(q, k_all, v_all, q_start)
    return o.reshape(b, L, ATTN_HEADS * ATTN_HEAD_DIM) @ w_o, k, v


def _trunk(x, c, pos0, cache_k, cache_v, page_table, ssm0, sc0, dw0, W):
    b = x.shape[0]
    dty = x.dtype
    cs = jax.nn.silu(c)
    ks_, vs_, ssm_, sc_, dw_ = [], [], [], [], []
    for l in range(DEPTH):
        mod = (cs @ W['w_ada'][l] + W['b_ada'][l]).reshape(b, N_SUB, 3, D_MODEL)
        h = _adaln(x, W['g_norm'][l, 0], mod[:, 0, 0], mod[:, 0, 1])
        f = _swiglu(h, W['w_ff_gate'][l, 0], W['w_ff_up'][l, 0], W['w_ff_down'][l, 0])
        x = x + MACARON * (1 + mod[:, 0, 2])[:, None] * f
        h = _adaln(x, W['g_norm'][l, 1], mod[:, 1, 0], mod[:, 1, 1])
        if l % 2 == 0:
            e = l // 2
            if ssm0 is None:
                h0 = jnp.zeros((b, SSD_HEADS, SSD_HEAD_DIM, SSD_STATE), dty)
                scb = jnp.zeros((b, SSD_CONV - 1, SSD_CONV_DIM), dty)
                dwb = jnp.zeros((b, CONF_KERNEL - 1, D_CONF), dty)
            else:
                h0, scb, dwb = ssm0[e], sc0[e], dw0[e]
            m, hN, scN, dwN = _ssd_conformer_mixer(h, W['w_in_even'][e], W['w_conv_ssd'][e], W['b_conv_ssd'][e], W['dt_bias'][e], W['a_log'][e], W['d_skip'][e], W['g_ssd_norm'][e], W['w_dw'][e], W['b_dw'][e], W['g_conf_ln'][e], W['b_conf_ln'][e], W['w_out_even'][e], h0, scb, dwb)
            ssm_.append(hN)
            sc_.append(scN)
            dw_.append(dwN)
        else:
            o = l // 2
            if cache_k is None:
                pk = pv = None
            else:
                pk = cache_k[o, page_table].reshape(b, -1, ATTN_HEADS, ATTN_HEAD_DIM)
                pv = cache_v[o, page_table].reshape(b, -1, ATTN_HEADS, ATTN_HEAD_DIM)
            m, kN, vN = _moba_mixer(h, W['w_qkv'][o], W['w_o'][o], pk, pv, pos0)
            ks_.append(kN)
            vs_.append(vN)
        x = x + (1 + mod[:, 1, 2])[:, None] * m
        h = _adaln(x, W['g_norm'][l, 2], mod[:, 2, 0], mod[:, 2, 1])
        f = _swiglu(h, W['w_ff_gate'][l, 1], W['w_ff_up'][l, 1], W['w_ff_down'][l, 1])
        x = x + MACARON * (1 + mod[:, 2, 2])[:, None] * f
    y = _rms_norm(x, W['g_final'])
    return y, jnp.stack(ks_), jnp.stack(vs_), jnp.stack(ssm_), jnp.stack(sc_), jnp.stack(dw_)


def setup_inputs(seed: int = 0) -> dict:
    key = jax.random.key(seed)
    keys = jax.random.split(key, 32)
    f32 = jnp.float32
    d = D_MODEL

    def nrm(i, shape, scale):
        return jax.random.normal(keys[i], shape, f32) * scale

    n_pages = PAST_LEN // PAGE_SIZE
    n_used = DEC_BATCH * n_pages
    n_phys = n_used + max(1, n_used // 4)
    page_table = jax.random.permutation(keys[0], n_phys)[:n_used].reshape(DEC_BATCH, n_pages).astype(jnp.int32)
    dt0 = jnp.exp(jax.random.uniform(keys[19], (N_EVEN, SSD_HEADS), f32, math.log(1e-3), math.log(1e-1)))
    return {
        'x_prompt': nrm(1, (BATCH, SEQ, d), 1.0),
        'x_sample': nrm(2, (DEC_BATCH, DEC_SEQ, d), 1.0),
        'cache_k': nrm(3, (N_ODD, n_phys, PAGE_SIZE, ATTN_HEADS, ATTN_HEAD_DIM), 1.0),
        'cache_v': nrm(4, (N_ODD, n_phys, PAGE_SIZE, ATTN_HEADS, ATTN_HEAD_DIM), 1.0),
        'page_table': page_table,
        'state_ssm': nrm(5, (N_EVEN, DEC_BATCH, SSD_HEADS, SSD_HEAD_DIM, SSD_STATE), 0.5),
        'state_ssd_conv': nrm(6, (N_EVEN, DEC_BATCH, SSD_CONV - 1, SSD_CONV_DIM), 1.0),
        'state_dwconv': nrm(7, (N_EVEN, DEC_BATCH, CONF_KERNEL - 1, D_CONF), 0.5),
        'c_prompt': nrm(8, (BATCH, d), 1.0),
        'c_sample': nrm(9, (DEC_BATCH, d), 1.0),
        'w_ada': nrm(10, (DEPTH, d, 3 * N_SUB * d), 0.1 * d ** -0.5),
        'b_ada': nrm(11, (DEPTH, 3 * N_SUB * d), 0.01),
        'g_norm': 1.0 + nrm(12, (DEPTH, N_SUB, d), 0.02),
        'w_ff_gate': nrm(13, (DEPTH, 2, d, D_FF), d ** -0.5),
        'w_ff_up': nrm(14, (DEPTH, 2, d, D_FF), d ** -0.5),
        'w_ff_down': nrm(15, (DEPTH, 2, D_FF, d), D_FF ** -0.5),
        'w_in_even': nrm(16, (N_EVEN, d, IN_EVEN), d ** -0.5),
        'w_conv_ssd': nrm(17, (N_EVEN, SSD_CONV, SSD_CONV_DIM), SSD_CONV ** -0.5),
        'b_conv_ssd': nrm(18, (N_EVEN, SSD_CONV_DIM), 0.01),
        'dt_bias': dt0 + jnp.log(-jnp.expm1(-dt0)),
        'a_log': jnp.log(jax.random.uniform(keys[20], (N_EVEN, SSD_HEADS), f32, 1.0, 16.0)),
        'd_skip': 1.0 + nrm(21, (N_EVEN, SSD_HEADS), 0.1),
        'g_ssd_norm': 1.0 + nrm(22, (N_EVEN, D_SSD), 0.02),
        'w_dw': nrm(23, (N_EVEN, CONF_KERNEL, D_CONF), CONF_KERNEL ** -0.5),
        'b_dw': nrm(24, (N_EVEN, D_CONF), 0.01),
        'g_conf_ln': 1.0 + nrm(25, (N_EVEN, D_CONF), 0.02),
        'b_conf_ln': nrm(26, (N_EVEN, D_CONF), 0.01),
        'w_out_even': nrm(27, (N_EVEN, D_SSD + D_CONF, d), (D_SSD + D_CONF) ** -0.5),
        'w_qkv': nrm(28, (N_ODD, d, 3 * ATTN_HEADS * ATTN_HEAD_DIM), d ** -0.5),
        'w_o': nrm(29, (N_ODD, ATTN_HEADS * ATTN_HEAD_DIM, d), (ATTN_HEADS * ATTN_HEAD_DIM) ** -0.5),
        'g_final': 1.0 + nrm(30, (d,), 0.02),
    }


def reference(x_prompt, x_sample, cache_k, cache_v, page_table, state_ssm, state_ssd_conv, state_dwconv, c_prompt, c_sample, w_ada, b_ada, g_norm, w_ff_gate, w_ff_up, w_ff_down, w_in_even, w_conv_ssd, b_conv_ssd, dt_bias, a_log, d_skip, g_ssd_norm, w_dw, b_dw, g_conf_ln, b_conf_ln, w_out_even, w_qkv, w_o, g_final):
    W = {'w_ada': w_ada, 'b_ada': b_ada, 'g_norm': g_norm, 'w_ff_gate': w_ff_gate, 'w_ff_up': w_ff_up, 'w_ff_down': w_ff_down, 'w_in_even': w_in_even, 'w_conv_ssd': w_conv_ssd, 'b_conv_ssd': b_conv_ssd, 'dt_bias': dt_bias, 'a_log': a_log, 'd_skip': d_skip, 'g_ssd_norm': g_ssd_norm, 'w_dw': w_dw, 'b_dw': b_dw, 'g_conf_ln': g_conf_ln, 'b_conf_ln': b_conf_ln, 'w_out_even': w_out_even, 'w_qkv': w_qkv, 'w_o': w_o, 'g_final': g_final}
    past_len = page_table.shape[1] * cache_k.shape[2]
    y_prompt, k_prompt, v_prompt, ssm_prompt, ssd_conv_prompt, dwconv_prompt = _trunk(x_prompt, c_prompt, 0, None, None, None, None, None, None, W)
    y_sample, k_sample, v_sample, ssm_sample, ssd_conv_sample, dwconv_sample = _trunk(x_sample, c_sample, past_len, cache_k, cache_v, page_table, state_ssm, state_ssd_conv, state_dwconv, W)
    return (y_prompt, y_sample, k_prompt, v_prompt, k_sample, v_sample, ssm_prompt, ssm_sample, ssd_conv_prompt, ssd_conv_sample, dwconv_prompt, dwconv_sample)
```

```python
import functools

import jax
import jax.numpy as jnp
from jax import lax
from jax.experimental import pallas as pl
from jax.experimental.pallas import tpu as pltpu

F32 = jnp.float32
BF16 = jnp.bfloat16

D_MODEL = 2048
DEPTH = 4
N_EVEN = (DEPTH + 1) // 2
N_ODD = DEPTH // 2
N_SUB = 3
MACARON = 0.5
D_FF = 5632
EPS = 1e-6
D_SSD = 2048
SSD_HEAD_DIM = 64
SSD_HEADS = D_SSD // SSD_HEAD_DIM
SSD_GROUPS = 4
SSD_STATE = 128
SSD_CONV = 4
SSD_CHUNK = 128
SSD_BC = SSD_GROUPS * SSD_STATE
SSD_CONV_DIM = D_SSD + 2 * SSD_BC
D_CONF = 2048
CONF_KERNEL = 31
ATTN_HEADS = 16
ATTN_HEAD_DIM = D_MODEL // ATTN_HEADS
MOBA_BLOCK = 256
MOBA_TOPK = 3
PAGE_SIZE = 128

LANES = 128
SUBLANES = 8
VMEM_LIMIT = 56 << 20
IN_Z, IN_A, IN_G, IN_XBC, IN_DT = 0, 2048, 4096, 6144, 9216
IN_PAD = 9728
DW_ROWS = 32
SC_ROWS = 8
NEG_INF = float("-inf")


def _cparams(sem):
    return pltpu.CompilerParams(dimension_semantics=sem, vmem_limit_bytes=VMEM_LIMIT)


def _dot(a, b):
    return jnp.dot(a, b, preferred_element_type=F32)


def _dot_nt(a, b):
    return lax.dot_general(a, b, (((1,), (1,)), ((), ())), preferred_element_type=F32)


def _silu(x):
    return x * jax.nn.sigmoid(x)


def _softplus(x):
    return jnp.maximum(x, 0.0) + jnp.log1p(jnp.exp(-jnp.abs(x)))


def _adaln(x, g, shift, scale):
    ms = jnp.mean(x * x, axis=-1, keepdims=True)
    y = x * lax.rsqrt(ms + EPS) * g
    return y * (1.0 + scale) + shift


def _split_bf16(x):
    hi = x.astype(BF16)
    lo = (x - hi.astype(F32)).astype(BF16)
    return hi, lo


def _ada_kernel(c_ref, w_ref, b_ref, o_ref):
    cs = _silu(c_ref[...]).astype(BF16)
    o_ref[...] = _dot(cs, w_ref[...].astype(BF16)) + b_ref[...]


def _ada(c_all, w_ada, b_ada):
    rows = c_all.shape[0]
    n = w_ada.shape[-1]
    tn = 1024
    return pl.pallas_call(
        _ada_kernel,
        grid=(DEPTH, n // tn),
        in_specs=[
            pl.BlockSpec((rows, D_MODEL), lambda l, j: (0, 0)),
            pl.BlockSpec((None, D_MODEL, tn), lambda l, j: (l, 0, j)),
            pl.BlockSpec((None, 1, tn), lambda l, j: (l, 0, j)),
        ],
        out_specs=pl.BlockSpec((None, rows, tn), lambda l, j: (l, 0, j)),
        out_shape=jax.ShapeDtypeStruct((DEPTH, rows, n), F32),
        compiler_params=_cparams(("parallel", "parallel")),
        name="ada_mod",
    )(c_all, w_ada, b_ada.reshape(DEPTH, 1, n))


class _Mod:
    def __init__(self, arr, rows_per_seq, per_row):
        self.arr = arr
        self.rows_per_seq = rows_per_seq
        self.per_row = per_row

    def spec(self, tm, width, tiled):
        if self.per_row:
            return pl.BlockSpec((3, tm, width), lambda m, n: (0, m, n if tiled else 0))
        bps = self.rows_per_seq // tm
        return pl.BlockSpec((None, 3, 1, width), lambda m, n: (m // bps, 0, 0, n if tiled else 0))


def _ln_mm_kernel(x_ref, g_ref, mod_ref, w_ref, o_ref, h_ref):
    @pl.when(pl.program_id(1) == 0)
    def _():
        h_ref[...] = _adaln(x_ref[...], g_ref[...], mod_ref[0], mod_ref[1]).astype(BF16)

    o_ref[...] = _dot(h_ref[...], w_ref[...])


def _ln_mm(x, g_norm3, gi, mod, w, wi, tm, tn):
    m_rows = x.shape[0]
    n = w.shape[-1]
    return pl.pallas_call(
        _ln_mm_kernel,
        grid=(m_rows // tm, n // tn),
        in_specs=[
            pl.BlockSpec((tm, D_MODEL), lambda m, j: (m, 0)),
            pl.BlockSpec((None, 1, D_MODEL), lambda m, j: (gi, 0, 0)),
            mod.spec(tm, D_MODEL, False),
            pl.BlockSpec((None, D_MODEL, tn), lambda m, j: (wi, 0, j)),
        ],
        out_specs=pl.BlockSpec((tm, tn), lambda m, j: (m, j)),
        out_shape=jax.ShapeDtypeStruct((m_rows, n), F32),
        scratch_shapes=[pltpu.VMEM((tm, D_MODEL), BF16)],
        compiler_params=_cparams(("parallel", "arbitrary")),
        name="ln_proj",
    )(x, g_norm3, mod.arr, w)


def _ffn_kernel(x_ref, g_ref, mod_ref, wg_ref, wu_ref, wd_ref, o_ref, h_ref, acc_ref, *, nf):
    f = pl.program_id(1)

    @pl.when(f == 0)
    def _():
        h_ref[...] = _adaln(x_ref[...], g_ref[...], mod_ref[0], mod_ref[1]).astype(BF16)
        acc_ref[...] = jnp.zeros_like(acc_ref)

    h = h_ref[...]
    a = (_silu(_dot(h, wg_ref[...])) * _dot(h, wu_ref[...])).astype(BF16)
    acc_ref[...] += _dot(a, wd_ref[...])

    @pl.when(f == nf - 1)
    def _():
        o_ref[...] = x_ref[...] + (MACARON * (1.0 + mod_ref[2])) * acc_ref[...]


def _ffn(x, g_norm3, gi, mod, wg, wu, wd, l, s, tm, tf):
    m_rows = x.shape[0]
    nf = D_FF // tf
    return pl.pallas_call(
        functools.partial(_ffn_kernel, nf=nf),
        grid=(m_rows // tm, nf),
        in_specs=[
            pl.BlockSpec((tm, D_MODEL), lambda m, f: (m, 0)),
            pl.BlockSpec((None, 1, D_MODEL), lambda m, f: (gi, 0, 0)),
            mod.spec(tm, D_MODEL, False),
            pl.BlockSpec((None, None, D_MODEL, tf), lambda m, f: (l, s, 0, f)),
            pl.BlockSpec((None, None, D_MODEL, tf), lambda m, f: (l, s, 0, f)),
            pl.BlockSpec((None, None, tf, D_MODEL), lambda m, f: (l, s, f, 0)),
        ],
        out_specs=pl.BlockSpec((tm, D_MODEL), lambda m, f: (m, 0)),
        out_shape=jax.ShapeDtypeStruct((m_rows, D_MODEL), F32),
        scratch_shapes=[pltpu.VMEM((tm, D_MODEL), BF16), pltpu.VMEM((tm, D_MODEL), F32)],
        compiler_params=_cparams(("parallel", "arbitrary")),
        name="ffn",
    )(x, g_norm3, mod.arr, wg, wu, wd)


def _mm_res_kernel(*refs, na):
    a_refs = refs[:na]
    w_refs = refs[na:2 * na]
    x_ref, mod_ref, o_ref = refs[2 * na:]
    acc = _dot(a_refs[0][...].astype(BF16), w_refs[0][...])
    for i in range(1, na):
        acc = acc + _dot(a_refs[i][...].astype(BF16), w_refs[i][...])
    o_ref[...] = x_ref[...] + (1.0 + mod_ref[2]) * acc


def _mm_res(a_list, w, wi, x, mod, tm, tn):
    m_rows = x.shape[0]
    na = len(a_list)
    ka = a_list[0].shape[1]
    in_specs = [pl.BlockSpec((tm, ka), lambda m, j: (m, 0)) for _ in a_list]
    in_specs += [pl.BlockSpec((None, None, ka, tn), lambda m, j, i=i: (wi, i, 0, j)) for i in range(na)]
    in_specs += [pl.BlockSpec((tm, tn), lambda m, j: (m, j)), mod.spec(tm, tn, True)]
    return pl.pallas_call(
        functools.partial(_mm_res_kernel, na=na),
        grid=(m_rows // tm, D_MODEL // tn),
        in_specs=in_specs,
        out_specs=pl.BlockSpec((tm, tn), lambda m, j: (m, j)),
        out_shape=jax.ShapeDtypeStruct((m_rows, D_MODEL), F32),
        compiler_params=_cparams(("parallel", "arbitrary")),
        name="mix_out",
    )(*a_list, *([w] * na), x, mod.arr)


def _cumsum_rows(v):
    rows = v.shape[0]
    row = lax.broadcasted_iota(jnp.int32, v.shape, 0)
    s = 1
    while s < rows:
        v = v + jnp.where(row >= s, pltpu.roll(v, s, axis=0), 0.0)
        s *= 2
    return v


def _expand_heads(v, e, passes):
    out = None
    r = v
    for _ in range(passes):
        hi = r.astype(BF16)
        t = _dot(hi, e)
        out = t if out is None else out + t
        r = r - hi.astype(F32)
    return out


def _ssd_kernel(z_ref, xbc_ref, dt_ref, wc_ref, bc_ref, dtb_ref, alog_ref, dsk_ref, gs_ref, e_ref,
                h0_ref, sc0_ref, y_ref, hn_ref, scn_ref, ht_ref, ext_ref, yacc_ref, *, nc, lv_last):
    q = SSD_CHUNK
    c = pl.program_id(1)

    @pl.when(c == 0)
    def _():
        ht_ref[...] = h0_ref[...].T
        ext_ref[0:SC_ROWS, :] = sc0_ref[...]

    ext_ref[SC_ROWS:SC_ROWS + q, :] = xbc_ref[...]
    base = SC_ROWS - (SSD_CONV - 1)
    acc = bc_ref[...] + wc_ref[0:1, :] * ext_ref[base:base + q, :]
    for k in range(1, SSD_CONV):
        acc = acc + wc_ref[k:k + 1, :] * ext_ref[base + k:base + k + q, :]
    xbc = _silu(acc)
    xs = xbc[:, :D_SSD]
    bm = xbc[:, D_SSD:D_SSD + SSD_BC]
    cm = xbc[:, D_SSD + SSD_BC:]

    dt = _softplus(dt_ref[...] + dtb_ref[...])
    if lv_last < q:
        row = lax.broadcasted_iota(jnp.int32, (q, LANES), 0)
        dt = jnp.where(row < jnp.where(c == nc - 1, lv_last, q), dt, 0.0)
    a_neg = -jnp.exp(alog_ref[...])
    a_cum = _cumsum_rows(dt * a_neg)
    a_cum_t = a_cum.T
    a_last = a_cum[q - 1:q, :]
    e = e_ref[...]
    xdt = xs * _expand_heads(dt, e, 2)
    xw = (xdt * _expand_heads(jnp.exp(a_last - a_cum), e, 2)).astype(BF16)
    expa_e = _expand_heads(jnp.exp(a_cum), e, 2)
    cd_e = _expand_heads(jnp.broadcast_to(jnp.exp(a_last), (SUBLANES, LANES)), e, 3)[0:1, :]
    qi = lax.broadcasted_iota(jnp.int32, (q, q), 0)
    si = lax.broadcasted_iota(jnp.int32, (q, q), 1)
    causal = qi >= si
    lane = lax.broadcasted_iota(jnp.int32, (q, LANES), 1)
    gw = D_SSD // SSD_GROUPS
    for g in range(SSD_GROUPS):
        bg = bm[:, g * SSD_STATE:(g + 1) * SSD_STATE]
        cg = cm[:, g * SSD_STATE:(g + 1) * SSD_STATE].astype(BF16)
        cb = _dot_nt(cg, bg.astype(BF16))
        gsl = slice(g * gw, (g + 1) * gw)
        htg = ht_ref[:, gsl]
        y_inter = _dot(cg, htg.astype(BF16)) * expa_e[:, gsl]
        ht_ref[:, gsl] = cd_e[:, gsl] * htg + _dot(bg.T.astype(BF16), xw[:, gsl])
        for pr in range(gw // LANES):
            lo = g * gw + pr * LANES
            xp = xdt[:, lo:lo + LANES]
            ypair = y_inter[:, pr * LANES:(pr + 1) * LANES]
            for half in range(2):
                h = lo // SSD_HEAD_DIM + half
                seg = a_cum[:, h:h + 1] - a_cum_t[h:h + 1, :]
                dec = jnp.exp(jnp.where(causal, seg, NEG_INF))
                mh = (cb * dec).astype(BF16)
                in_half = (lane >= SSD_HEAD_DIM) if half else (lane < SSD_HEAD_DIM)
                ypair = ypair + _dot(mh, jnp.where(in_half, xp, 0.0).astype(BF16))
            yacc_ref[:, lo:lo + LANES] = ypair
    y = (yacc_ref[...] + dsk_ref[...] * xs) * _silu(z_ref[...])
    for g in range(SSD_GROUPS):
        gsl = slice(g * gw, (g + 1) * gw)
        yg = y[:, gsl]
        ms = jnp.mean(yg * yg, axis=-1, keepdims=True)
        y_ref[:, gsl] = (yg * lax.rsqrt(ms + EPS) * gs_ref[:, gsl]).astype(y_ref.dtype)

    @pl.when(c == nc - 1)
    def _():
        hn_ref[...] = ht_ref[...].T
        scn_ref[...] = ext_ref[lv_last:lv_last + SC_ROWS, :]

    @pl.when(c < nc - 1)
    def _():
        ext_ref[0:SC_ROWS, :] = ext_ref[q:q + SC_ROWS, :]


def _ssd(proj, nb, nc, lv_last, e, p, e_mat, h0, h0_e, sc0, sc0_e):
    q = SSD_CHUNK
    rows = proj.shape[0]
    vec = lambda width: pl.BlockSpec((None, 1, width), lambda b, c: (e, 0, 0))
    return pl.pallas_call(
        functools.partial(_ssd_kernel, nc=nc, lv_last=lv_last),
        grid=(nb, nc),
        in_specs=[
            pl.BlockSpec((q, D_SSD), lambda b, c: (b * nc + c, IN_Z // D_SSD)),
            pl.BlockSpec((q, SSD_CONV_DIM), lambda b, c: (b * nc + c, IN_XBC // SSD_CONV_DIM)),
            pl.BlockSpec((q, LANES), lambda b, c: (b * nc + c, IN_DT // LANES)),
            pl.BlockSpec((None, SSD_CONV, SSD_CONV_DIM), lambda b, c: (e, 0, 0)),
            vec(SSD_CONV_DIM), vec(LANES), vec(LANES), vec(D_SSD), vec(D_SSD),
            pl.BlockSpec((LANES, D_SSD), lambda b, c: (0, 0)),
            pl.BlockSpec((None, None, D_SSD, SSD_STATE), lambda b, c: (h0_e, b, 0, 0)),
            pl.BlockSpec((None, None, SC_ROWS, SSD_CONV_DIM), lambda b, c: (sc0_e, b, 0, 0)),
        ],
        out_specs=[
            pl.BlockSpec((q, D_SSD), lambda b, c: (b * nc + c, 0)),
            pl.BlockSpec((None, D_SSD, SSD_STATE), lambda b, c: (b, 0, 0)),
            pl.BlockSpec((None, SC_ROWS, SSD_CONV_DIM), lambda b, c: (b, 0, 0)),
        ],
        out_shape=[
            jax.ShapeDtypeStruct((rows, D_SSD), BF16),
            jax.ShapeDtypeStruct((nb, D_SSD, SSD_STATE), F32),
            jax.ShapeDtypeStruct((nb, SC_ROWS, SSD_CONV_DIM), F32),
        ],
        scratch_shapes=[
            pltpu.VMEM((SSD_STATE, D_SSD), F32),
            pltpu.VMEM((SC_ROWS + q, SSD_CONV_DIM), F32),
            pltpu.VMEM((q, D_SSD), F32),
        ],
        compiler_params=_cparams(("parallel", "arbitrary")),
        name="ssd_scan",
    )(proj, proj, proj, p["w_conv"], p["b_conv"], p["dt_bias"], p["a_log"], p["d_skip"], p["g_ssd"],
      e_mat, h0, sc0)


def _conf_kernel(a_ref, gt_ref, wdw_ref, bdw_ref, lng_ref, lnb_ref, dw0_ref, u_ref, dwn_ref,
                 ext_ref, cv_ref, *, nc, lv_last):
    q = SSD_CHUNK
    c = pl.program_id(1)

    @pl.when(c == 0)
    def _():
        ext_ref[0:DW_ROWS, :] = dw0_ref[...]

    ext_ref[DW_ROWS:DW_ROWS + q, :] = a_ref[...] * jax.nn.sigmoid(gt_ref[...])
    base = DW_ROWS - (CONF_KERNEL - 1)
    ct = 256
    for t in range(D_CONF // ct):
        cs = slice(t * ct, (t + 1) * ct)
        acc = bdw_ref[:, cs] + wdw_ref[0:1, cs] * ext_ref[base:base + q, cs]
        for k in range(1, CONF_KERNEL):
            acc = acc + wdw_ref[k:k + 1, cs] * ext_ref[base + k:base + k + q, cs]
        cv_ref[:, cs] = acc
    v = cv_ref[...]
    mu = jnp.mean(v, axis=-1, keepdims=True)
    xc = v - mu
    var = jnp.mean(xc * xc, axis=-1, keepdims=True)
    yn = xc * lax.rsqrt(var + EPS) * lng_ref[...] + lnb_ref[...]
    u_ref[...] = _silu(yn).astype(u_ref.dtype)

    @pl.when(c == nc - 1)
    def _():
        dwn_ref[...] = ext_ref[lv_last:lv_last + DW_ROWS, :]

    @pl.when(c < nc - 1)
    def _():
        ext_ref[0:DW_ROWS, :] = ext_ref[q:q + DW_ROWS, :]


def _conf(proj, nb, nc, lv_last, e, p, dw0, dw0_e):
    q = SSD_CHUNK
    rows = proj.shape[0]
    vec = lambda: pl.BlockSpec((None, 1, D_CONF), lambda b, c: (e, 0, 0))
    return pl.pallas_call(
        functools.partial(_conf_kernel, nc=nc, lv_last=lv_last),
        grid=(nb, nc),
        in_specs=[
            pl.BlockSpec((q, D_CONF), lambda b, c: (b * nc + c, IN_A // D_CONF)),
            pl.BlockSpec((q, D_CONF), lambda b, c: (b * nc + c, IN_G // D_CONF)),
            pl.BlockSpec((None, DW_ROWS, D_CONF), lambda b, c: (e, 0, 0)),
            vec(), vec(), vec(),
            pl.BlockSpec((None, None, DW_ROWS, D_CONF), lambda b, c: (dw0_e, b, 0, 0)),
        ],
        out_specs=[
            pl.BlockSpec((q, D_CONF), lambda b, c: (b * nc + c, 0)),
            pl.BlockSpec((None, DW_ROWS, D_CONF), lambda b, c: (b, 0, 0)),
        ],
        out_shape=[
            jax.ShapeDtypeStruct((rows, D_CONF), BF16),
            jax.ShapeDtypeStruct((nb, DW_ROWS, D_CONF), F32),
        ],
        scratch_shapes=[
            pltpu.VMEM((DW_ROWS + q, D_CONF), F32),
            pltpu.VMEM((q, D_CONF), F32),
        ],
        compiler_params=_cparams(("parallel", "arbitrary")),
        name="conf_conv",
    )(proj, proj, p["w_dw"], p["b_dw"], p["ln_g"], p["ln_b"], dw0)


def _moba_prompt_kernel(q_ref, k_ref, v_ref, o_ref, s_ref, *, seq):
    blk = MOBA_BLOCK
    nb = seq // blk
    scale = ATTN_HEAD_DIM ** -0.5
    q = q_ref[...]
    k = k_ref[...]
    km = jnp.mean(k.reshape(nb, blk, ATTN_HEAD_DIM), axis=1)
    qh, ql = _split_bf16(q)
    kh, kl = _split_bf16(km)
    sc_t = _dot_nt(kh, qh) + _dot_nt(kh, ql) + _dot_nt(kl, qh)
    kb = k.astype(BF16)
    v_t = v_ref[...].T.astype(BF16)
    sub = lax.broadcasted_iota(jnp.int32, (nb, blk), 0)
    kidx = lax.broadcasted_iota(jnp.int32, (blk, blk), 0)
    qidx = lax.broadcasted_iota(jnp.int32, (blk, blk), 1)
    for j in range(nb):
        sc = sc_t[:, j * blk:(j + 1) * blk]
        rank = jnp.zeros((nb, blk), F32)
        for i in range(j):
            r = sc[i:i + 1, :]
            beats = jnp.where(r > sc, 1.0, jnp.where(r == sc, jnp.where(sub > i, 1.0, 0.0), 0.0))
            rank = rank + beats
        bias = jnp.where(rank < MOBA_TOPK, jnp.where(sub < j, 0.0, NEG_INF), NEG_INF)
        qj = qh[j * blk:(j + 1) * blk, :]
        for i in range(j + 1):
            s = _dot_nt(kb[i * blk:(i + 1) * blk, :], qj) * scale
            if i < j:
                s = s + bias[i:i + 1, :]
            else:
                s = jnp.where(kidx <= qidx, s, NEG_INF)
            s_ref[i * blk:(i + 1) * blk, :] = s
        nk = (j + 1) * blk
        s_all = s_ref[0:nk, :]
        m = jnp.max(s_all, axis=0, keepdims=True)
        p = jnp.exp(s_all - m)
        l = jnp.sum(p, axis=0, keepdims=True)
        o_t = _dot(v_t[:, 0:nk], p.astype(BF16)) / l
        o_ref[j * blk:(j + 1) * blk, :] = o_t.T.astype(o_ref.dtype)


def _moba_prompt(qkv, nb, seq):
    hd = ATTN_HEAD_DIM
    return pl.pallas_call(
        functools.partial(_moba_prompt_kernel, seq=seq),
        grid=(nb, ATTN_HEADS),
        in_specs=[
            pl.BlockSpec((seq, hd), lambda b, h: (b, h)),
            pl.BlockSpec((seq, hd), lambda b, h: (b, ATTN_HEADS + h)),
            pl.BlockSpec((seq, hd), lambda b, h: (b, 2 * ATTN_HEADS + h)),
        ],
        out_specs=pl.BlockSpec((seq, hd), lambda b, h: (b, h)),
        out_shape=jax.ShapeDtypeStruct((nb * seq, D_MODEL), BF16),
        scratch_shapes=[pltpu.VMEM((seq, MOBA_BLOCK), F32)],
        compiler_params=_cparams(("parallel", "parallel")),
        name="moba_prompt",
    )(qkv, qkv, qkv)


PAGES_PER_STEP = 8
BLOCKS_PER_STEP = PAGES_PER_STEP * PAGE_SIZE // MOBA_BLOCK


def _block_diag_queries(q8):
    nq = q8.shape[0]
    rows = ATTN_HEADS * nq
    qrep = jnp.concatenate([q8] * ATTN_HEADS, axis=0)
    rh = lax.broadcasted_iota(jnp.int32, (rows, D_MODEL), 0) // nq
    ch = lax.broadcasted_iota(jnp.int32, (rows, D_MODEL), 1) // ATTN_HEAD_DIM
    return jnp.where(rh == ch, qrep, 0.0), rh == ch


def _sample_scores_kernel(pt_ref, q_ref, *refs, nsteps):
    k_refs = refs[:PAGES_PER_STEP]
    s_ref, bs_ref = refs[PAGES_PER_STEP:]
    st = pl.program_id(1)
    scale = ATTN_HEAD_DIM ** -0.5
    qbd, _ = _block_diag_queries(q_ref[...])
    qbd = qbd.astype(BF16)

    @pl.when(st == 0)
    def _():
        bs_ref[...] = jnp.zeros_like(bs_ref)

    bs = bs_ref[...]
    lane = lax.broadcasted_iota(jnp.int32, bs.shape, 1)
    ppb = MOBA_BLOCK // PAGE_SIZE
    for j in range(BLOCKS_PER_STEP):
        kblk = jnp.concatenate([k_refs[ppb * j + r][...] for r in range(ppb)], axis=0).astype(BF16)
        sj = _dot_nt(qbd, kblk) * scale
        s_ref[j] = sj
        bmean = jnp.sum(sj, axis=-1, keepdims=True) * (1.0 / MOBA_BLOCK)
        bs = jnp.where(lane == st * BLOCKS_PER_STEP + j, bmean, bs)
    bs_ref[...] = bs


def _top_blocks(bs, nvalid):
    lane = lax.broadcasted_iota(jnp.int32, bs.shape, 1).astype(F32)
    cur = jnp.where(lane < nvalid, bs, NEG_INF)
    sel = jnp.zeros(bs.shape, F32)
    for _ in range(MOBA_TOPK):
        m = jnp.max(cur, axis=-1, keepdims=True)
        idx = jnp.min(jnp.where(cur == m, lane, 1e9), axis=-1, keepdims=True)
        pick = jnp.where(lane == idx, jnp.where(m > NEG_INF, 1.0, 0.0), 0.0)
        sel = jnp.maximum(sel, pick)
        cur = jnp.where(pick > 0.5, NEG_INF, cur)
    return sel


def _sample_attn_kernel(pt_ref, s_ref, bs_ref, q_ref, kn_ref, vn_ref, *refs, nsteps, nblocks):
    v_refs = refs[:PAGES_PER_STEP]
    o_ref, p_ref, acc_ref, l_ref = refs[PAGES_PER_STEP:]
    st = pl.program_id(1)
    nq = q_ref.shape[0]
    rows = ATTN_HEADS * nq
    scale = ATTN_HEAD_DIM ** -0.5

    @pl.when(st == 0)
    def _():
        sel = _top_blocks(bs_ref[...], nblocks)
        lane = lax.broadcasted_iota(jnp.int32, (rows, LANES), 1)
        qbd, _ = _block_diag_queries(q_ref[...])
        pad = jnp.zeros((LANES - nq, D_MODEL), F32)
        kn = jnp.concatenate([kn_ref[...], pad], axis=0).astype(BF16)
        vn = jnp.concatenate([vn_ref[...], pad], axis=0).astype(BF16)
        s_own = _dot_nt(qbd.astype(BF16), kn) * scale
        qpos = lax.broadcasted_iota(jnp.int32, (rows, LANES), 0) % nq
        s_own = jnp.where(lane <= qpos, s_own, NEG_INF)

        def sel_col(n):
            return jnp.max(jnp.where(lane == n, sel, 0.0), axis=-1, keepdims=True) > 0.5

        def max_body(n, m):
            sn = jnp.where(sel_col(n), s_ref[n], NEG_INF)
            return jnp.maximum(m, jnp.max(sn, axis=-1, keepdims=True))

        m = lax.fori_loop(0, nblocks, max_body, jnp.max(s_own, axis=-1, keepdims=True))
        p_own = jnp.exp(s_own - m)

        def p_body(n, l):
            pn = jnp.where(sel_col(n), jnp.exp(s_ref[n] - m), 0.0)
            p_ref[n] = pn.astype(BF16)
            return l + jnp.sum(pn, axis=-1, keepdims=True)

        l = lax.fori_loop(0, nblocks, p_body, jnp.sum(p_own, axis=-1, keepdims=True))
        l_ref[...] = jnp.broadcast_to(l, l_ref.shape)
        acc_ref[...] = _dot(p_own.astype(BF16), vn)

    ppb = MOBA_BLOCK // PAGE_SIZE
    acc = acc_ref[...]
    for j in range(BLOCKS_PER_STEP):
        vblk = jnp.concatenate([v_refs[ppb * j + r][...] for r in range(ppb)], axis=0).astype(BF16)
        acc = acc + _dot(p_ref[st * BLOCKS_PER_STEP + j], vblk)
    acc_ref[...] = acc

    @pl.when(st == nsteps - 1)
    def _():
        rh = lax.broadcasted_iota(jnp.int32, (rows, D_MODEL), 0) // nq
        ch = lax.broadcasted_iota(jnp.int32, (rows, D_MODEL), 1) // ATTN_HEAD_DIM
        o_full = jnp.where(rh == ch, acc_ref[...] / l_ref[:, 0:1], 0.0)
        o_ref[...] = jnp.sum(o_full.reshape(ATTN_HEADS, nq, D_MODEL), axis=0)


def _page_specs(o, n_pages_seq):
    specs = []
    for r in range(PAGES_PER_STEP):
        specs.append(pl.BlockSpec(
            (None, None, PAGE_SIZE, D_MODEL),
            lambda b, st, pt, r=r: (o, pt[b * n_pages_seq + st * PAGES_PER_STEP + r], 0, 0)))
    return specs


def _moba_sample(qkv, cache_k, cache_v, pt_flat, o, nseq, nq):
    n_pages_seq = pt_flat.shape[0] // nseq
    nblocks = n_pages_seq * PAGE_SIZE // MOBA_BLOCK
    nsteps = n_pages_seq // PAGES_PER_STEP
    rows = ATTN_HEADS * nq
    s_all, bscore = pl.pallas_call(
        functools.partial(_sample_scores_kernel, nsteps=nsteps),
        grid_spec=pltpu.PrefetchScalarGridSpec(
            num_scalar_prefetch=1,
            grid=(nseq, nsteps),
            in_specs=[pl.BlockSpec((nq, D_MODEL), lambda b, st, pt: (b, 0))] + _page_specs(o, n_pages_seq),
            out_specs=[
                pl.BlockSpec((None, BLOCKS_PER_STEP, rows, MOBA_BLOCK), lambda b, st, pt: (b, st, 0, 0)),
                pl.BlockSpec((None, rows, LANES), lambda b, st, pt: (b, 0, 0)),
            ],
        ),
        out_shape=[
            jax.ShapeDtypeStruct((nseq, nblocks, rows, MOBA_BLOCK), F32),
            jax.ShapeDtypeStruct((nseq, rows, LANES), F32),
        ],
        compiler_params=_cparams(("parallel", "arbitrary")),
        name="moba_sample_scores",
    )(pt_flat, qkv, *([cache_k] * PAGES_PER_STEP))
    return pl.pallas_call(
        functools.partial(_sample_attn_kernel, nsteps=nsteps, nblocks=nblocks),
        grid_spec=pltpu.PrefetchScalarGridSpec(
            num_scalar_prefetch=1,
            grid=(nseq, nsteps),
            in_specs=[
                pl.BlockSpec((None, nblocks, rows, MOBA_BLOCK), lambda b, st, pt: (b, 0, 0, 0)),
                pl.BlockSpec((None, rows, LANES), lambda b, st, pt: (b, 0, 0)),
                pl.BlockSpec((nq, D_MODEL), lambda b, st, pt: (b, 0)),
                pl.BlockSpec((nq, D_MODEL), lambda b, st, pt: (b, 1)),
                pl.BlockSpec((nq, D_MODEL), lambda b, st, pt: (b, 2)),
            ] + _page_specs(o, n_pages_seq),
            out_specs=pl.BlockSpec((nq, D_MODEL), lambda b, st, pt: (b, 0)),
            scratch_shapes=[
                pltpu.VMEM((nblocks, rows, MOBA_BLOCK), BF16),
                pltpu.VMEM((rows, D_MODEL), F32),
                pltpu.VMEM((rows, LANES), F32),
            ],
        ),
        out_shape=jax.ShapeDtypeStruct((nseq * nq, D_MODEL), F32),
        compiler_params=_cparams(("parallel", "arbitrary")),
        name="moba_sample_attn",
    )(pt_flat, s_all, bscore, qkv, qkv, qkv, *([cache_v] * PAGES_PER_STEP))


def _rms_kernel(x_ref, g_ref, o_ref):
    x = x_ref[...]
    ms = jnp.mean(x * x, axis=-1, keepdims=True)
    o_ref[...] = x * lax.rsqrt(ms + EPS) * g_ref[...]


def _final_norm(x, g, tm):
    m_rows = x.shape[0]
    return pl.pallas_call(
        _rms_kernel,
        grid=(m_rows // tm,),
        in_specs=[pl.BlockSpec((tm, D_MODEL), lambda m: (m, 0)), pl.BlockSpec((1, D_MODEL), lambda m: (0, 0))],
        out_specs=pl.BlockSpec((tm, D_MODEL), lambda m: (m, 0)),
        out_shape=jax.ShapeDtypeStruct((m_rows, D_MODEL), F32),
        compiler_params=_cparams(("parallel",)),
        name="final_norm",
    )(x, g.reshape(1, D_MODEL))


def _trunk(x, nseq, seq, mods, W, tiles, cache=None, states=None):
    tm_proj, tm_ffn, tm_out = tiles
    q = SSD_CHUNK
    seq_pad = -(-seq // q) * q
    nc = seq_pad // q
    lv_last = seq - (nc - 1) * q
    ks, vs, ssm, scs, dws = [], [], [], [], []
    for l in range(DEPTH):
        x = _ffn(x, W["g_norm"], l * N_SUB, mods[l][0], W["wg"], W["wu"], W["wd"], l, 0, tm_ffn, 512)
        if l % 2 == 0:
            e = l // 2
            proj = _ln_mm(x, W["g_norm"], l * N_SUB + 1, mods[l][1], W["w_in"], e, tm_proj, 512)
            if seq_pad != seq:
                proj = jnp.pad(proj.reshape(nseq, seq, IN_PAD), ((0, 0), (0, seq_pad - seq), (0, 0)))
                proj = proj.reshape(nseq * seq_pad, IN_PAD)
            if states is None:
                h0, h0_e, sc0, sc0_e, dw0, dw0_e = W["zero_h"], 0, W["zero_sc"], 0, W["zero_dw"], 0
            else:
                h0, sc0, dw0 = states
                h0_e = sc0_e = dw0_e = e
            y, hn, scn = _ssd(proj, nseq, nc, lv_last, e, W["ssd"], W["e_mat"], h0, h0_e, sc0, sc0_e)
            u, dwn = _conf(proj, nseq, nc, lv_last, e, W["conf"], dw0, dw0_e)
            if seq_pad != seq:
                y = y.reshape(nseq, seq_pad, D_SSD)[:, :seq].reshape(nseq * seq, D_SSD)
                u = u.reshape(nseq, seq_pad, D_CONF)[:, :seq].reshape(nseq * seq, D_CONF)
            ssm.append(hn.reshape(nseq, SSD_HEADS, SSD_HEAD_DIM, SSD_STATE))
            scs.append(scn[:, SC_ROWS - (SSD_CONV - 1):])
            dws.append(dwn[:, DW_ROWS - (CONF_KERNEL - 1):])
            x = _mm_res([y, u], W["w_out"], e, x, mods[l][1], tm_out, 512)
        else:
            o = l // 2
            qkv = _ln_mm(x, W["g_norm"], l * N_SUB + 1, mods[l][1], W["w_qkv"], o, tm_proj, 512)
            ks.append(qkv[:, D_MODEL:2 * D_MODEL].reshape(nseq, seq, ATTN_HEADS, ATTN_HEAD_DIM))
            vs.append(qkv[:, 2 * D_MODEL:].reshape(nseq, seq, ATTN_HEADS, ATTN_HEAD_DIM))
            if cache is None:
                att = _moba_prompt(qkv, nseq, seq)
            else:
                cache_k, cache_v, pt_flat = cache
                att = _moba_sample(qkv, cache_k, cache_v, pt_flat, o, nseq, seq)
            x = _mm_res([att], W["w_o"], o, x, mods[l][1], tm_out, 512)
        x = _ffn(x, W["g_norm"], l * N_SUB + 2, mods[l][2], W["wg"], W["wu"], W["wd"], l, 1, tm_ffn, 512)
    y = _final_norm(x, W["g_final"], min(x.shape[0], 512))
    return (y.reshape(nseq, seq, D_MODEL), jnp.stack(ks), jnp.stack(vs), jnp.stack(ssm), jnp.stack(scs),
            jnp.stack(dws))


def kernel(x_prompt, x_sample, cache_k, cache_v, page_table, state_ssm, state_ssd_conv, state_dwconv, c_prompt, c_sample, w_ada, b_ada, g_norm, w_ff_gate, w_ff_up, w_ff_down, w_in_even, w_conv_ssd, b_conv_ssd, dt_bias, a_log, d_skip, g_ssd_norm, w_dw, b_dw, g_conf_ln, b_conf_ln, w_out_even, w_qkv, w_o, g_final):
    nb_p, seq_p, _ = x_prompt.shape
    nb_s, seq_s, _ = x_sample.shape
    d = D_MODEL

    i1, i2, i3 = D_SSD, D_SSD + SSD_CONV_DIM, D_SSD + SSD_CONV_DIM + SSD_HEADS
    w_in = jnp.concatenate([
        w_in_even[..., :i1], w_in_even[..., i3:], w_in_even[..., i1:i2], w_in_even[..., i2:i3],
        jnp.zeros((N_EVEN, d, IN_PAD - IN_DT - SSD_HEADS), w_in_even.dtype)], axis=-1).astype(BF16)
    pad_h = lambda a: jnp.pad(a, ((0, 0), (0, LANES - SSD_HEADS))).reshape(N_EVEN, 1, LANES)
    lane_head = jnp.arange(D_SSD)[None, :] // SSD_HEAD_DIM
    W = {
        "g_norm": g_norm.reshape(DEPTH * N_SUB, 1, d),
        "wg": w_ff_gate.astype(BF16), "wu": w_ff_up.astype(BF16), "wd": w_ff_down.astype(BF16),
        "w_in": w_in,
        "w_out": w_out_even.astype(BF16).reshape(N_EVEN, 2, D_SSD, d),
        "w_qkv": w_qkv.astype(BF16),
        "w_o": w_o.astype(BF16).reshape(N_ODD, 1, d, d),
        "g_final": g_final,
        "e_mat": (jnp.arange(LANES)[:, None] == lane_head).astype(BF16),
        "ssd": {
            "w_conv": w_conv_ssd, "b_conv": b_conv_ssd.reshape(N_EVEN, 1, SSD_CONV_DIM),
            "dt_bias": pad_h(dt_bias), "a_log": pad_h(a_log),
            "d_skip": jnp.repeat(d_skip, SSD_HEAD_DIM, axis=-1).reshape(N_EVEN, 1, D_SSD),
            "g_ssd": g_ssd_norm.reshape(N_EVEN, 1, D_SSD),
        },
        "conf": {
            "w_dw": jnp.pad(w_dw, ((0, 0), (0, DW_ROWS - CONF_KERNEL), (0, 0))),
            "b_dw": b_dw.reshape(N_EVEN, 1, D_CONF),
            "ln_g": g_conf_ln.reshape(N_EVEN, 1, D_CONF), "ln_b": b_conf_ln.reshape(N_EVEN, 1, D_CONF),
        },
        "zero_h": jnp.zeros((1, nb_p, D_SSD, SSD_STATE), F32),
        "zero_sc": jnp.zeros((1, nb_p, SC_ROWS, SSD_CONV_DIM), F32),
        "zero_dw": jnp.zeros((1, nb_p, DW_ROWS, D_CONF), F32),
    }

    n_c = nb_p + nb_s
    c_rows = -(-n_c // SUBLANES) * SUBLANES
    c_all = jnp.pad(jnp.concatenate([c_prompt, c_sample], axis=0), ((0, c_rows - n_c), (0, 0)))
    mod_all = _ada(c_all, w_ada, b_ada).reshape(DEPTH, c_rows, N_SUB, 3, d)
    mods_p, mods_s = [], []
    for l in range(DEPTH):
        mp, ms = [], []
        for s in range(N_SUB):
            mp.append(_Mod(mod_all[l, :nb_p, s][:, :, None, :], seq_p, False))
            rows = jnp.repeat(mod_all[l, nb_p:n_c, s], seq_s, axis=0)
            ms.append(_Mod(jnp.transpose(rows, (1, 0, 2)), seq_s, True))
        mods_p.append(mp)
        mods_s.append(ms)

    out_p = _trunk(x_prompt.reshape(nb_p * seq_p, d), nb_p, seq_p, mods_p, W, (1024, 512, 1024))

    n_phys = cache_k.shape[1]
    ck = cache_k.reshape(N_ODD, n_phys, PAGE_SIZE, d)
    cv = cache_v.reshape(N_ODD, n_phys, PAGE_SIZE, d)
    states = (
        state_ssm.reshape(N_EVEN, nb_s, D_SSD, SSD_STATE),
        jnp.pad(state_ssd_conv, ((0, 0), (0, 0), (SC_ROWS - (SSD_CONV - 1), 0), (0, 0))),
        jnp.pad(state_dwconv, ((0, 0), (0, 0), (DW_ROWS - (CONF_KERNEL - 1), 0), (0, 0))),
    )
    m_s = nb_s * seq_s
    out_s = _trunk(x_sample.reshape(m_s, d), nb_s, seq_s, mods_s, W, (m_s, m_s, m_s),
                   cache=(ck, cv, page_table.reshape(-1)), states=states)

    y_p, k_p, v_p, ssm_p, sc_p, dw_p = out_p
    y_s, k_s, v_s, ssm_s, sc_s, dw_s = out_s
    return (y_p, y_s, k_p, v_p, k_s, v_s, ssm_p, ssm_s, sc_p, sc_s, dw_p, dw_s)
```

```python
import functools

import jax
import jax.numpy as jnp
from jax import lax
from jax.experimental import pallas as pl
from jax.experimental.pallas import tpu as pltpu

F32 = jnp.float32
BF16 = jnp.bfloat16

D_MODEL = 2048
DEPTH = 4
N_EVEN = (DEPTH + 1) // 2
N_ODD = DEPTH // 2
N_SUB = 3
MACARON = 0.5
D_FF = 5632
EPS = 1e-6
D_SSD = 2048
SSD_HEAD_DIM = 64
SSD_HEADS = D_SSD // SSD_HEAD_DIM
SSD_GROUPS = 4
SSD_STATE = 128
SSD_CONV = 4
SSD_CHUNK = 128
SSD_BC = SSD_GROUPS * SSD_STATE
SSD_CONV_DIM = D_SSD + 2 * SSD_BC
D_CONF = 2048
CONF_KERNEL = 31
ATTN_HEADS = 16
ATTN_HEAD_DIM = D_MODEL // ATTN_HEADS
MOBA_BLOCK = 256
MOBA_TOPK = 3
PAGE_SIZE = 128

LANES = 128
SUBLANES = 8
VMEM_LIMIT = 56 << 20
IN_Z, IN_A, IN_G, IN_XBC, IN_DT = 0, 2048, 4096, 6144, 9216
IN_PAD = 9728
DW_ROWS = 32
SC_ROWS = 8
NEG_INF = float("-inf")


def _cparams(sem):
    return pltpu.CompilerParams(dimension_semantics=sem, vmem_limit_bytes=VMEM_LIMIT)


def _dot(a, b):
    return jnp.dot(a, b, preferred_element_type=F32)


def _dot_nt(a, b):
    return lax.dot_general(a, b, (((1,), (1,)), ((), ())), preferred_element_type=F32)


def _silu(x):
    return x * jax.nn.sigmoid(x)


def _softplus(x):
    return jnp.maximum(x, 0.0) + jnp.log1p(jnp.exp(-jnp.abs(x)))


def _adaln(x, g, shift, scale):
    ms = jnp.mean(x * x, axis=-1, keepdims=True)
    y = x * lax.rsqrt(ms + EPS) * g
    return y * (1.0 + scale) + shift


def _split_bf16(x):
    hi = x.astype(BF16)
    lo = (x - hi.astype(F32)).astype(BF16)
    return hi, lo


def _ada_kernel(c_ref, w_ref, b_ref, o_ref):
    cs = _silu(c_ref[...]).astype(BF16)
    o_ref[...] = _dot(cs, w_ref[...].astype(BF16)) + b_ref[...]


def _ada(c_all, w_ada, b_ada):
    rows = c_all.shape[0]
    n = w_ada.shape[-1]
    tn = 1024
    return pl.pallas_call(
        _ada_kernel,
        grid=(DEPTH, n // tn),
        in_specs=[
            pl.BlockSpec((rows, D_MODEL), lambda l, j: (0, 0)),
            pl.BlockSpec((None, D_MODEL, tn), lambda l, j: (l, 0, j)),
            pl.BlockSpec((None, 1, tn), lambda l, j: (l, 0, j)),
        ],
        out_specs=pl.BlockSpec((None, rows, tn), lambda l, j: (l, 0, j)),
        out_shape=jax.ShapeDtypeStruct((DEPTH, rows, n), F32),
        compiler_params=_cparams(("parallel", "parallel")),
        name="ada_mod",
    )(c_all, w_ada, b_ada.reshape(DEPTH, 1, n))


class _Mod:
    def __init__(self, arr, rows_per_seq, per_row):
        self.arr = arr
        self.rows_per_seq = rows_per_seq
        self.per_row = per_row

    def spec(self, tm, width, tiled):
        if self.per_row:
            return pl.BlockSpec((3, tm, width), lambda m, n: (0, m, n if tiled else 0))
        bps = self.rows_per_seq // tm
        return pl.BlockSpec((None, 3, 1, width), lambda m, n: (m // bps, 0, 0, n if tiled else 0))


def _ln_mm_kernel(x_ref, g_ref, mod_ref, w_ref, *rest, n_out, tiles_per_out):
    o_refs, h_ref = rest[:n_out], rest[n_out]
    j = pl.program_id(1)

    @pl.when(j == 0)
    def _():
        h_ref[...] = _adaln(x_ref[...], g_ref[...], mod_ref[0], mod_ref[1]).astype(BF16)

    if n_out == 1:
        o_refs[0][...] = _dot(h_ref[...], w_ref[...])
    for i in range(n_out if n_out > 1 else 0):
        @pl.when(j // tiles_per_out == i)
        def _(i=i):
            o_refs[i][...] = _dot(h_ref[...], w_ref[...])


def _ln_mm(x, g_norm3, gi, mod, w, wi, tm, tn, n_out=1):
    m_rows = x.shape[0]
    n = w.shape[-1]
    tpo = n // n_out // tn
    out_specs = [pl.BlockSpec((tm, tn), lambda m, j, i=i: (m, jnp.clip(j - i * tpo, 0, tpo - 1)))
                 for i in range(n_out)]
    outs = pl.pallas_call(
        functools.partial(_ln_mm_kernel, n_out=n_out, tiles_per_out=tpo),
        grid=(m_rows // tm, n // tn),
        in_specs=[
            pl.BlockSpec((tm, D_MODEL), lambda m, j: (m, 0)),
            pl.BlockSpec((None, 1, D_MODEL), lambda m, j: (gi, 0, 0)),
            mod.spec(tm, D_MODEL, False),
            pl.BlockSpec((None, D_MODEL, tn), lambda m, j: (wi, 0, j)),
        ],
        out_specs=out_specs,
        out_shape=[jax.ShapeDtypeStruct((m_rows, n // n_out), F32) for _ in range(n_out)],
        scratch_shapes=[pltpu.VMEM((tm, D_MODEL), BF16)],
        compiler_params=_cparams(("parallel", "arbitrary")),
        name="ln_proj",
    )(x, g_norm3, mod.arr, w)
    return outs[0] if n_out == 1 else outs


def _ffn_kernel(x_ref, g_ref, mod_ref, wg_ref, wu_ref, wd_ref, gf_ref, o_ref, h_ref, *, nf, final_norm):
    f = pl.program_id(1)

    @pl.when(f == 0)
    def _():
        h_ref[...] = _adaln(x_ref[...], g_ref[...], mod_ref[0], mod_ref[1]).astype(BF16)
        o_ref[...] = jnp.zeros_like(o_ref)

    h = h_ref[...]
    a = (_silu(_dot(h, wg_ref[...])) * _dot(h, wu_ref[...])).astype(BF16)
    o_ref[...] += _dot(a, wd_ref[...])

    @pl.when(f == nf - 1)
    def _():
        y = x_ref[...] + (MACARON * (1.0 + mod_ref[2])) * o_ref[...]
        if final_norm:
            ms = jnp.mean(y * y, axis=-1, keepdims=True)
            y = y * lax.rsqrt(ms + EPS) * gf_ref[...]
        o_ref[...] = y


def _ffn(x, g_norm3, gi, mod, wg, wu, wd, l, s, tm, tf, g_final, final_norm=False):
    m_rows = x.shape[0]
    nf = D_FF // tf
    return pl.pallas_call(
        functools.partial(_ffn_kernel, nf=nf, final_norm=final_norm),
        grid=(m_rows // tm, nf),
        in_specs=[
            pl.BlockSpec((tm, D_MODEL), lambda m, f: (m, 0), pipeline_mode=pl.Buffered(1)),
            pl.BlockSpec((None, 1, D_MODEL), lambda m, f: (gi, 0, 0)),
            mod.spec(tm, D_MODEL, False),
            pl.BlockSpec((None, None, D_MODEL, tf), lambda m, f: (l, s, 0, f)),
            pl.BlockSpec((None, None, D_MODEL, tf), lambda m, f: (l, s, 0, f)),
            pl.BlockSpec((None, None, tf, D_MODEL), lambda m, f: (l, s, f, 0)),
            pl.BlockSpec((1, D_MODEL), lambda m, f: (0, 0)),
        ],
        out_specs=pl.BlockSpec((tm, D_MODEL), lambda m, f: (m, 0)),
        out_shape=jax.ShapeDtypeStruct((m_rows, D_MODEL), F32),
        scratch_shapes=[pltpu.VMEM((tm, D_MODEL), BF16)],
        compiler_params=_cparams(("parallel", "arbitrary")),
        name="ffn",
    )(x, g_norm3, mod.arr, wg, wu, wd, g_final)


def _mm_res_kernel(*refs, na):
    a_refs = refs[:na]
    w_refs = refs[na:2 * na]
    x_ref, mod_ref, o_ref = refs[2 * na:]
    acc = _dot(a_refs[0][...].astype(BF16), w_refs[0][...])
    for i in range(1, na):
        acc = acc + _dot(a_refs[i][...].astype(BF16), w_refs[i][...])
    o_ref[...] = x_ref[...] + (1.0 + mod_ref[2]) * acc


def _mm_res(a_list, w, wi, x, mod, tm, tn):
    m_rows = x.shape[0]
    na = len(a_list)
    ka = a_list[0].shape[1]
    in_specs = [pl.BlockSpec((tm, ka), lambda m, j: (m, 0)) for _ in a_list]
    in_specs += [pl.BlockSpec((None, None, ka, tn), lambda m, j, i=i: (wi, i, 0, j)) for i in range(na)]
    in_specs += [pl.BlockSpec((tm, tn), lambda m, j: (m, j)), mod.spec(tm, tn, True)]
    return pl.pallas_call(
        functools.partial(_mm_res_kernel, na=na),
        grid=(m_rows // tm, D_MODEL // tn),
        in_specs=in_specs,
        out_specs=pl.BlockSpec((tm, tn), lambda m, j: (m, j)),
        out_shape=jax.ShapeDtypeStruct((m_rows, D_MODEL), F32),
        compiler_params=_cparams(("parallel", "arbitrary")),
        name="mix_out",
    )(*a_list, *([w] * na), x, mod.arr)


def _cumsum_rows(v):
    rows = v.shape[0]
    row = lax.broadcasted_iota(jnp.int32, v.shape, 0)
    s = 1
    while s < rows:
        v = v + jnp.where(row >= s, pltpu.roll(v, s, axis=0), 0.0)
        s *= 2
    return v


def _expand_heads(v, e, passes):
    out = None
    r = v
    for _ in range(passes):
        hi = r.astype(BF16)
        t = _dot(hi, e)
        out = t if out is None else out + t
        r = r - hi.astype(F32)
    return out


def _ssd_kernel(z_ref, xbc_ref, dt_ref, wc_ref, bc_ref, dtb_ref, alog_ref, dsk_ref, gs_ref, e_ref,
                h0_ref, sc0_ref, y_ref, hn_ref, scn_ref, ht_ref, ext_ref, yacc_ref, *, nc, lv_last):
    q = SSD_CHUNK
    c = pl.program_id(1)

    @pl.when(c == 0)
    def _():
        ht_ref[...] = h0_ref[...].T
        ext_ref[0:SC_ROWS, :] = sc0_ref[...]

    ext_ref[SC_ROWS:SC_ROWS + q, :] = xbc_ref[...]
    base = SC_ROWS - (SSD_CONV - 1)
    acc = bc_ref[...] + wc_ref[0:1, :] * ext_ref[base:base + q, :]
    for k in range(1, SSD_CONV):
        acc = acc + wc_ref[k:k + 1, :] * ext_ref[base + k:base + k + q, :]
    xbc = _silu(acc)
    xs = xbc[:, :D_SSD]
    bm = xbc[:, D_SSD:D_SSD + SSD_BC]
    cm = xbc[:, D_SSD + SSD_BC:]

    dt = _softplus(dt_ref[...] + dtb_ref[...])
    if lv_last < q:
        row = lax.broadcasted_iota(jnp.int32, (q, LANES), 0)
        dt = jnp.where(row < jnp.where(c == nc - 1, lv_last, q), dt, 0.0)
    a_neg = -jnp.exp(alog_ref[...])
    a_cum = _cumsum_rows(dt * a_neg)
    a_cum_t = a_cum.T
    a_last = a_cum[q - 1:q, :]
    e = e_ref[...]
    xdt = xs * _expand_heads(dt, e, 2)
    xw = (xdt * _expand_heads(jnp.exp(a_last - a_cum), e, 2)).astype(BF16)
    expa_e = _expand_heads(jnp.exp(a_cum), e, 2)
    cd_e = _expand_heads(jnp.broadcast_to(jnp.exp(a_last), (SUBLANES, LANES)), e, 3)[0:1, :]
    qi = lax.broadcasted_iota(jnp.int32, (q, q), 0)
    si = lax.broadcasted_iota(jnp.int32, (q, q), 1)
    causal = qi >= si
    lane = lax.broadcasted_iota(jnp.int32, (q, LANES), 1)
    gw = D_SSD // SSD_GROUPS
    for g in range(SSD_GROUPS):
        bg = bm[:, g * SSD_STATE:(g + 1) * SSD_STATE]
        cg = cm[:, g * SSD_STATE:(g + 1) * SSD_STATE].astype(BF16)
        cb = _dot_nt(cg, bg.astype(BF16))
        gsl = slice(g * gw, (g + 1) * gw)
        htg = ht_ref[:, gsl]
        y_inter = _dot(cg, htg.astype(BF16)) * expa_e[:, gsl]
        ht_ref[:, gsl] = cd_e[:, gsl] * htg + _dot(bg.T.astype(BF16), xw[:, gsl])
        for pr in range(gw // LANES):
            lo = g * gw + pr * LANES
            xp = xdt[:, lo:lo + LANES]
            ypair = y_inter[:, pr * LANES:(pr + 1) * LANES]
            for half in range(2):
                h = lo // SSD_HEAD_DIM + half
                seg = a_cum[:, h:h + 1] - a_cum_t[h:h + 1, :]
                dec = jnp.exp(jnp.where(causal, seg, NEG_INF))
                mh = (cb * dec).astype(BF16)
                in_half = (lane >= SSD_HEAD_DIM) if half else (lane < SSD_HEAD_DIM)
                ypair = ypair + _dot(mh, jnp.where(in_half, xp, 0.0).astype(BF16))
            yacc_ref[:, lo:lo + LANES] = ypair
    y = (yacc_ref[...] + dsk_ref[...] * xs) * _silu(z_ref[...])
    for g in range(SSD_GROUPS):
        gsl = slice(g * gw, (g + 1) * gw)
        yg = y[:, gsl]
        ms = jnp.mean(yg * yg, axis=-1, keepdims=True)
        y_ref[:, gsl] = (yg * lax.rsqrt(ms + EPS) * gs_ref[:, gsl]).astype(y_ref.dtype)

    @pl.when(c == nc - 1)
    def _():
        hn_ref[...] = ht_ref[...].T
        scn_ref[...] = ext_ref[lv_last:lv_last + SC_ROWS, :]

    @pl.when(c < nc - 1)
    def _():
        ext_ref[0:SC_ROWS, :] = ext_ref[q:q + SC_ROWS, :]


def _ssd(proj, nb, nc, lv_last, e, p, e_mat, h0, h0_e, sc0, sc0_e):
    q = SSD_CHUNK
    rows = proj.shape[0]
    vec = lambda width: pl.BlockSpec((None, 1, width), lambda b, c: (e, 0, 0))
    return pl.pallas_call(
        functools.partial(_ssd_kernel, nc=nc, lv_last=lv_last),
        grid=(nb, nc),
        in_specs=[
            pl.BlockSpec((q, D_SSD), lambda b, c: (b * nc + c, IN_Z // D_SSD)),
            pl.BlockSpec((q, SSD_CONV_DIM), lambda b, c: (b * nc + c, IN_XBC // SSD_CONV_DIM)),
            pl.BlockSpec((q, LANES), lambda b, c: (b * nc + c, IN_DT // LANES)),
            pl.BlockSpec((None, SSD_CONV, SSD_CONV_DIM), lambda b, c: (e, 0, 0)),
            vec(SSD_CONV_DIM), vec(LANES), vec(LANES), vec(D_SSD), vec(D_SSD),
            pl.BlockSpec((LANES, D_SSD), lambda b, c: (0, 0)),
            pl.BlockSpec((None, None, D_SSD, SSD_STATE), lambda b, c: (h0_e, b, 0, 0)),
            pl.BlockSpec((None, None, SC_ROWS, SSD_CONV_DIM), lambda b, c: (sc0_e, b, 0, 0)),
        ],
        out_specs=[
            pl.BlockSpec((q, D_SSD), lambda b, c: (b * nc + c, 0)),
            pl.BlockSpec((None, D_SSD, SSD_STATE), lambda b, c: (b, 0, 0)),
            pl.BlockSpec((None, SC_ROWS, SSD_CONV_DIM), lambda b, c: (b, 0, 0)),
        ],
        out_shape=[
            jax.ShapeDtypeStruct((rows, D_SSD), BF16),
            jax.ShapeDtypeStruct((nb, D_SSD, SSD_STATE), F32),
            jax.ShapeDtypeStruct((nb, SC_ROWS, SSD_CONV_DIM), F32),
        ],
        scratch_shapes=[
            pltpu.VMEM((SSD_STATE, D_SSD), F32),
            pltpu.VMEM((SC_ROWS + q, SSD_CONV_DIM), F32),
            pltpu.VMEM((q, D_SSD), F32),
        ],
        compiler_params=_cparams(("parallel", "arbitrary")),
        name="ssd_scan",
    )(proj, proj, proj, p["w_conv"], p["b_conv"], p["dt_bias"], p["a_log"], p["d_skip"], p["g_ssd"],
      e_mat, h0, sc0)


def _conf_kernel(a_ref, gt_ref, wdw_ref, bdw_ref, lng_ref, lnb_ref, dw0_ref, u_ref, dwn_ref,
                 ext_ref, cv_ref, sh_ref, *, nc, lv_last):
    q = SSD_CHUNK
    c = pl.program_id(1)

    @pl.when(c == 0)
    def _():
        ext_ref[0:DW_ROWS, :] = dw0_ref[...]

    ext_ref[DW_ROWS:DW_ROWS + q, :] = a_ref[...] * jax.nn.sigmoid(gt_ref[...])
    base = DW_ROWS - (CONF_KERNEL - 1)
    sh_rows = sh_ref.shape[1]
    for r in range(1, SUBLANES):
        sh_ref[r - 1] = ext_ref[r:r + sh_rows, :]
    ct = 256
    for t in range(D_CONF // ct):
        cs = slice(t * ct, (t + 1) * ct)
        acc = bdw_ref[:, cs]
        for k in range(CONF_KERNEL):
            r, al = (base + k) % SUBLANES, (base + k) // SUBLANES * SUBLANES
            win = ext_ref[al:al + q, cs] if r == 0 else sh_ref[r - 1, al:al + q, cs]
            acc = acc + wdw_ref[k:k + 1, cs] * win
        cv_ref[:, cs] = acc
    v = cv_ref[...]
    mu = jnp.mean(v, axis=-1, keepdims=True)
    xc = v - mu
    var = jnp.mean(xc * xc, axis=-1, keepdims=True)
    yn = xc * lax.rsqrt(var + EPS) * lng_ref[...] + lnb_ref[...]
    u_ref[...] = _silu(yn).astype(u_ref.dtype)

    @pl.when(c == nc - 1)
    def _():
        dwn_ref[...] = ext_ref[lv_last:lv_last + DW_ROWS, :]

    @pl.when(c < nc - 1)
    def _():
        ext_ref[0:DW_ROWS, :] = ext_ref[q:q + DW_ROWS, :]


def _conf(proj, nb, nc, lv_last, e, p, dw0, dw0_e):
    q = SSD_CHUNK
    rows = proj.shape[0]
    vec = lambda: pl.BlockSpec((None, 1, D_CONF), lambda b, c: (e, 0, 0))
    return pl.pallas_call(
        functools.partial(_conf_kernel, nc=nc, lv_last=lv_last),
        grid=(nb, nc),
        in_specs=[
            pl.BlockSpec((q, D_CONF), lambda b, c: (b * nc + c, IN_A // D_CONF)),
            pl.BlockSpec((q, D_CONF), lambda b, c: (b * nc + c, IN_G // D_CONF)),
            pl.BlockSpec((None, DW_ROWS, D_CONF), lambda b, c: (e, 0, 0)),
            vec(), vec(), vec(),
            pl.BlockSpec((None, None, DW_ROWS, D_CONF), lambda b, c: (dw0_e, b, 0, 0)),
        ],
        out_specs=[
            pl.BlockSpec((q, D_CONF), lambda b, c: (b * nc + c, 0)),
            pl.BlockSpec((None, DW_ROWS, D_CONF), lambda b, c: (b, 0, 0)),
        ],
        out_shape=[
            jax.ShapeDtypeStruct((rows, D_CONF), BF16),
            jax.ShapeDtypeStruct((nb, DW_ROWS, D_CONF), F32),
        ],
        scratch_shapes=[
            pltpu.VMEM((DW_ROWS + q, D_CONF), F32),
            pltpu.VMEM((q, D_CONF), F32),
            pltpu.VMEM((SUBLANES - 1, DW_ROWS + q - SUBLANES, D_CONF), F32),
        ],
        compiler_params=_cparams(("parallel", "arbitrary")),
        name="conf_conv",
    )(proj, proj, p["w_dw"], p["b_dw"], p["ln_g"], p["ln_b"], dw0)


def _moba_prompt_kernel(q_ref, k_ref, v_ref, o_ref, s_ref, *, seq):
    blk = MOBA_BLOCK
    nb = seq // blk
    scale = ATTN_HEAD_DIM ** -0.5
    q = q_ref[...]
    k = k_ref[...]
    km = jnp.mean(k.reshape(nb, blk, ATTN_HEAD_DIM), axis=1)
    qh, ql = _split_bf16(q)
    kh, kl = _split_bf16(km)
    sc_t = _dot_nt(kh, qh) + _dot_nt(kh, ql) + _dot_nt(kl, qh)
    qs = (q * scale).astype(BF16)
    kb = k.astype(BF16)
    v_t = v_ref[...].T.astype(BF16)
    sub = lax.broadcasted_iota(jnp.int32, (nb, blk), 0)
    kidx = lax.broadcasted_iota(jnp.int32, (blk, blk), 0)
    qidx = lax.broadcasted_iota(jnp.int32, (blk, blk), 1)
    for j in range(nb):
        sc = sc_t[:, j * blk:(j + 1) * blk]
        rank = jnp.zeros((nb, blk), F32)
        for i in range(j):
            r = sc[i:i + 1, :]
            beats = jnp.where(r > sc, 1.0, jnp.where(r == sc, jnp.where(sub > i, 1.0, 0.0), 0.0))
            rank = rank + beats
        bias = jnp.where(rank < MOBA_TOPK, jnp.where(sub < j, 0.0, NEG_INF), NEG_INF)
        qj = qs[j * blk:(j + 1) * blk, :]
        for i in range(j + 1):
            s = _dot_nt(kb[i * blk:(i + 1) * blk, :], qj)
            if i < j:
                s = s + bias[i:i + 1, :]
            else:
                s = jnp.where(kidx <= qidx, s, NEG_INF)
            s_ref[i * blk:(i + 1) * blk, :] = s
        nk = (j + 1) * blk
        s_all = s_ref[0:nk, :]
        m = jnp.max(s_all, axis=0, keepdims=True)
        p = jnp.exp(s_all - m)
        l = jnp.sum(p, axis=0, keepdims=True)
        o_t = _dot(v_t[:, 0:nk], p.astype(BF16)) / l
        o_ref[j * blk:(j + 1) * blk, :] = o_t.T.astype(o_ref.dtype)


def _moba_prompt(q, k, v, nb, seq):
    spec = pl.BlockSpec((seq, ATTN_HEAD_DIM), lambda b, h: (b, h))
    return pl.pallas_call(
        functools.partial(_moba_prompt_kernel, seq=seq),
        grid=(nb, ATTN_HEADS),
        in_specs=[spec, spec, spec],
        out_specs=spec,
        out_shape=jax.ShapeDtypeStruct((nb * seq, D_MODEL), BF16),
        scratch_shapes=[pltpu.VMEM((seq, MOBA_BLOCK), F32)],
        compiler_params=_cparams(("parallel", "parallel")),
        name="moba_prompt",
    )(q, k, v)


PAGES_PER_STEP = 8
BLOCKS_PER_STEP = PAGES_PER_STEP * PAGE_SIZE // MOBA_BLOCK


def _load_page(ref):
    return jnp.concatenate(
        [ref[pl.ds(h, PAGE_SIZE, stride=ATTN_HEADS), :] for h in range(ATTN_HEADS)], axis=1)


def _block_diag_queries(q8):
    nq = q8.shape[0]
    rows = ATTN_HEADS * nq
    qrep = jnp.concatenate([q8] * ATTN_HEADS, axis=0)
    rh = lax.broadcasted_iota(jnp.int32, (rows, D_MODEL), 0) // nq
    ch = lax.broadcasted_iota(jnp.int32, (rows, D_MODEL), 1) // ATTN_HEAD_DIM
    return jnp.where(rh == ch, qrep, 0.0), rh == ch


def _sample_scores_kernel(pt_ref, q_ref, *refs, nsteps):
    k_refs = refs[:PAGES_PER_STEP]
    s_ref, bs_ref = refs[PAGES_PER_STEP:]
    st = pl.program_id(1)
    scale = ATTN_HEAD_DIM ** -0.5
    qbd, _ = _block_diag_queries(q_ref[...])
    qbd = qbd.astype(BF16)

    @pl.when(st == 0)
    def _():
        bs_ref[...] = jnp.zeros_like(bs_ref)

    bs = bs_ref[...]
    lane = lax.broadcasted_iota(jnp.int32, bs.shape, 1)
    ppb = MOBA_BLOCK // PAGE_SIZE
    for j in range(BLOCKS_PER_STEP):
        kblk = jnp.concatenate([_load_page(k_refs[ppb * j + r]) for r in range(ppb)], axis=0).astype(BF16)
        sj = _dot_nt(qbd, kblk) * scale
        s_ref[j] = sj
        bmean = jnp.sum(sj, axis=-1, keepdims=True) * (1.0 / MOBA_BLOCK)
        bs = jnp.where(lane == st * BLOCKS_PER_STEP + j, bmean, bs)
    bs_ref[...] = bs


def _top_blocks(bs, nvalid):
    lane = lax.broadcasted_iota(jnp.int32, bs.shape, 1).astype(F32)
    cur = jnp.where(lane < nvalid, bs, NEG_INF)
    sel = jnp.zeros(bs.shape, F32)
    for _ in range(MOBA_TOPK):
        m = jnp.max(cur, axis=-1, keepdims=True)
        idx = jnp.min(jnp.where(cur == m, lane, 1e9), axis=-1, keepdims=True)
        pick = jnp.where(lane == idx, jnp.where(m > NEG_INF, 1.0, 0.0), 0.0)
        sel = jnp.maximum(sel, pick)
        cur = jnp.where(pick > 0.5, NEG_INF, cur)
    return sel


def _sample_attn_kernel(pt_ref, s_ref, bs_ref, q_ref, kn_ref, vn_ref, *refs, nsteps, nblocks):
    v_refs = refs[:PAGES_PER_STEP]
    o_ref, p_ref, acc_ref, l_ref = refs[PAGES_PER_STEP:]
    st = pl.program_id(1)
    nq = q_ref.shape[0]
    rows = ATTN_HEADS * nq
    scale = ATTN_HEAD_DIM ** -0.5

    @pl.when(st == 0)
    def _():
        sel = _top_blocks(bs_ref[...], nblocks)
        lane = lax.broadcasted_iota(jnp.int32, (rows, LANES), 1)
        qbd, _ = _block_diag_queries(q_ref[...])
        pad = jnp.zeros((LANES - nq, D_MODEL), F32)
        kn = jnp.concatenate([kn_ref[...], pad], axis=0).astype(BF16)
        vn = jnp.concatenate([vn_ref[...], pad], axis=0).astype(BF16)
        s_own = _dot_nt(qbd.astype(BF16), kn) * scale
        qpos = lax.broadcasted_iota(jnp.int32, (rows, LANES), 0) % nq
        s_own = jnp.where(lane <= qpos, s_own, NEG_INF)

        def sel_col(n):
            return jnp.max(jnp.where(lane == n, sel, 0.0), axis=-1, keepdims=True) > 0.5

        def max_body(n, m):
            sn = jnp.where(sel_col(n), s_ref[n], NEG_INF)
            return jnp.maximum(m, jnp.max(sn, axis=-1, keepdims=True))

        m = lax.fori_loop(0, nblocks, max_body, jnp.max(s_own, axis=-1, keepdims=True))
        p_own = jnp.exp(s_own - m)

        def p_body(n, l):
            pn = jnp.where(sel_col(n), jnp.exp(s_ref[n] - m), 0.0)
            p_ref[n] = pn.astype(BF16)
            return l + jnp.sum(pn, axis=-1, keepdims=True)

        l = lax.fori_loop(0, nblocks, p_body, jnp.sum(p_own, axis=-1, keepdims=True))
        l_ref[...] = jnp.broadcast_to(l, l_ref.shape)
        acc_ref[...] = _dot(p_own.astype(BF16), vn)

    ppb = MOBA_BLOCK // PAGE_SIZE
    acc = acc_ref[...]
    for j in range(BLOCKS_PER_STEP):
        vblk = jnp.concatenate([_load_page(v_refs[ppb * j + r]) for r in range(ppb)], axis=0).astype(BF16)
        acc = acc + _dot(p_ref[st * BLOCKS_PER_STEP + j], vblk)
    acc_ref[...] = acc

    @pl.when(st == nsteps - 1)
    def _():
        rh = lax.broadcasted_iota(jnp.int32, (rows, D_MODEL), 0) // nq
        ch = lax.broadcasted_iota(jnp.int32, (rows, D_MODEL), 1) // ATTN_HEAD_DIM
        o_full = jnp.where(rh == ch, acc_ref[...] / l_ref[:, 0:1], 0.0)
        o_ref[...] = jnp.sum(o_full.reshape(ATTN_HEADS, nq, D_MODEL), axis=0)


def _page_specs(o, n_pages_seq):
    specs = []
    for r in range(PAGES_PER_STEP):
        specs.append(pl.BlockSpec(
            (None, None, PAGE_SIZE * ATTN_HEADS, ATTN_HEAD_DIM),
            lambda b, st, pt, r=r: (o, pt[b * n_pages_seq + st * PAGES_PER_STEP + r], 0, 0)))
    return specs


def _moba_sample(q, k_new, v_new, cache_k, cache_v, pt_flat, o, nseq, nq):
    n_pages_seq = pt_flat.shape[0] // nseq
    nblocks = n_pages_seq * PAGE_SIZE // MOBA_BLOCK
    nsteps = n_pages_seq // PAGES_PER_STEP
    rows = ATTN_HEADS * nq
    s_all, bscore = pl.pallas_call(
        functools.partial(_sample_scores_kernel, nsteps=nsteps),
        grid_spec=pltpu.PrefetchScalarGridSpec(
            num_scalar_prefetch=1,
            grid=(nseq, nsteps),
            in_specs=[pl.BlockSpec((nq, D_MODEL), lambda b, st, pt: (b, 0))] + _page_specs(o, n_pages_seq),
            out_specs=[
                pl.BlockSpec((None, BLOCKS_PER_STEP, rows, MOBA_BLOCK), lambda b, st, pt: (b, st, 0, 0)),
                pl.BlockSpec((None, rows, LANES), lambda b, st, pt: (b, 0, 0)),
            ],
        ),
        out_shape=[
            jax.ShapeDtypeStruct((nseq, nblocks, rows, MOBA_BLOCK), F32),
            jax.ShapeDtypeStruct((nseq, rows, LANES), F32),
        ],
        compiler_params=_cparams(("parallel", "arbitrary")),
        name="moba_sample_scores",
    )(pt_flat, q, *([cache_k] * PAGES_PER_STEP))
    return pl.pallas_call(
        functools.partial(_sample_attn_kernel, nsteps=nsteps, nblocks=nblocks),
        grid_spec=pltpu.PrefetchScalarGridSpec(
            num_scalar_prefetch=1,
            grid=(nseq, nsteps),
            in_specs=[
                pl.BlockSpec((None, nblocks, rows, MOBA_BLOCK), lambda b, st, pt: (b, 0, 0, 0)),
                pl.BlockSpec((None, rows, LANES), lambda b, st, pt: (b, 0, 0)),
                pl.BlockSpec((nq, D_MODEL), lambda b, st, pt: (b, 0)),
                pl.BlockSpec((nq, D_MODEL), lambda b, st, pt: (b, 0)),
                pl.BlockSpec((nq, D_MODEL), lambda b, st, pt: (b, 0)),
            ] + _page_specs(o, n_pages_seq),
            out_specs=pl.BlockSpec((nq, D_MODEL), lambda b, st, pt: (b, 0)),
            scratch_shapes=[
                pltpu.VMEM((nblocks, rows, MOBA_BLOCK), BF16),
                pltpu.VMEM((rows, D_MODEL), F32),
                pltpu.VMEM((rows, LANES), F32),
            ],
        ),
        out_shape=jax.ShapeDtypeStruct((nseq * nq, D_MODEL), F32),
        compiler_params=_cparams(("parallel", "arbitrary")),
        name="moba_sample_attn",
    )(pt_flat, s_all, bscore, q, k_new, v_new, *([cache_v] * PAGES_PER_STEP))


def _trunk(x, nseq, seq, mods, W, tiles, cache=None, states=None):
    tm_proj, tm_ffn, tm_out, tm_qkv = tiles
    q = SSD_CHUNK
    seq_pad = -(-seq // q) * q
    nc = seq_pad // q
    lv_last = seq - (nc - 1) * q
    ks, vs, ssm, scs, dws = [], [], [], [], []
    ffn = functools.partial(_ffn, wg=W["wg"], wu=W["wu"], wd=W["wd"], tm=tm_ffn, tf=512, g_final=W["g_final"])
    for l in range(DEPTH):
        x = ffn(x, W["g_norm"], l * N_SUB, mods[l][0], l=l, s=0)
        if l % 2 == 0:
            e = l // 2
            proj = _ln_mm(x, W["g_norm"], l * N_SUB + 1, mods[l][1], W["w_in"], e, tm_proj, 512)
            if seq_pad != seq:
                proj = jnp.pad(proj.reshape(nseq, seq, IN_PAD), ((0, 0), (0, seq_pad - seq), (0, 0)))
                proj = proj.reshape(nseq * seq_pad, IN_PAD)
            if states is None:
                h0, h0_e, sc0, sc0_e, dw0, dw0_e = W["zero_h"], 0, W["zero_sc"], 0, W["zero_dw"], 0
            else:
                h0, sc0, dw0 = states
                h0_e = sc0_e = dw0_e = e
            y, hn, scn = _ssd(proj, nseq, nc, lv_last, e, W["ssd"], W["e_mat"], h0, h0_e, sc0, sc0_e)
            u, dwn = _conf(proj, nseq, nc, lv_last, e, W["conf"], dw0, dw0_e)
            if seq_pad != seq:
                y = y.reshape(nseq, seq_pad, D_SSD)[:, :seq].reshape(nseq * seq, D_SSD)
                u = u.reshape(nseq, seq_pad, D_CONF)[:, :seq].reshape(nseq * seq, D_CONF)
            ssm.append(hn.reshape(nseq, SSD_HEADS, SSD_HEAD_DIM, SSD_STATE))
            scs.append(scn[:, SC_ROWS - (SSD_CONV - 1):])
            dws.append(dwn[:, DW_ROWS - (CONF_KERNEL - 1):])
            x = _mm_res([y, u], W["w_out"], e, x, mods[l][1], tm_out, 512)
        else:
            o = l // 2
            qm, km, vm = _ln_mm(x, W["g_norm"], l * N_SUB + 1, mods[l][1], W["w_qkv"], o, tm_qkv, 1024, n_out=3)
            ks.append(km.reshape(nseq, seq, ATTN_HEADS, ATTN_HEAD_DIM))
            vs.append(vm.reshape(nseq, seq, ATTN_HEADS, ATTN_HEAD_DIM))
            if cache is None:
                att = _moba_prompt(qm, km, vm, nseq, seq)
            else:
                cache_k, cache_v, pt_flat = cache
                att = _moba_sample(qm, km, vm, cache_k, cache_v, pt_flat, o, nseq, seq)
            x = _mm_res([att], W["w_o"], o, x, mods[l][1], tm_out, 512)
        x = ffn(x, W["g_norm"], l * N_SUB + 2, mods[l][2], l=l, s=1, final_norm=(l == DEPTH - 1))
    return (x.reshape(nseq, seq, D_MODEL), jnp.stack(ks), jnp.stack(vs), jnp.stack(ssm), jnp.stack(scs),
            jnp.stack(dws))


def kernel(x_prompt, x_sample, cache_k, cache_v, page_table, state_ssm, state_ssd_conv, state_dwconv, c_prompt, c_sample, w_ada, b_ada, g_norm, w_ff_gate, w_ff_up, w_ff_down, w_in_even, w_conv_ssd, b_conv_ssd, dt_bias, a_log, d_skip, g_ssd_norm, w_dw, b_dw, g_conf_ln, b_conf_ln, w_out_even, w_qkv, w_o, g_final):
    nb_p, seq_p, _ = x_prompt.shape
    nb_s, seq_s, _ = x_sample.shape
    d = D_MODEL

    i1, i2, i3 = D_SSD, D_SSD + SSD_CONV_DIM, D_SSD + SSD_CONV_DIM + SSD_HEADS
    w_in = jnp.concatenate([
        w_in_even[..., :i1], w_in_even[..., i3:], w_in_even[..., i1:i2], w_in_even[..., i2:i3],
        jnp.zeros((N_EVEN, d, IN_PAD - IN_DT - SSD_HEADS), w_in_even.dtype)], axis=-1).astype(BF16)
    pad_h = lambda a: jnp.pad(a, ((0, 0), (0, LANES - SSD_HEADS))).reshape(N_EVEN, 1, LANES)
    lane_head = jnp.arange(D_SSD)[None, :] // SSD_HEAD_DIM
    W = {
        "g_norm": g_norm.reshape(DEPTH * N_SUB, 1, d),
        "wg": w_ff_gate.astype(BF16), "wu": w_ff_up.astype(BF16), "wd": w_ff_down.astype(BF16),
        "w_in": w_in,
        "w_out": w_out_even.astype(BF16).reshape(N_EVEN, 2, D_SSD, d),
        "w_qkv": w_qkv.astype(BF16),
        "w_o": w_o.astype(BF16).reshape(N_ODD, 1, d, d),
        "g_final": g_final.reshape(1, d),
        "e_mat": (jnp.arange(LANES)[:, None] == lane_head).astype(BF16),
        "ssd": {
            "w_conv": w_conv_ssd, "b_conv": b_conv_ssd.reshape(N_EVEN, 1, SSD_CONV_DIM),
            "dt_bias": pad_h(dt_bias), "a_log": pad_h(a_log),
            "d_skip": jnp.repeat(d_skip, SSD_HEAD_DIM, axis=-1).reshape(N_EVEN, 1, D_SSD),
            "g_ssd": g_ssd_norm.reshape(N_EVEN, 1, D_SSD),
        },
        "conf": {
            "w_dw": jnp.pad(w_dw, ((0, 0), (0, DW_ROWS - CONF_KERNEL), (0, 0))),
            "b_dw": b_dw.reshape(N_EVEN, 1, D_CONF),
            "ln_g": g_conf_ln.reshape(N_EVEN, 1, D_CONF), "ln_b": b_conf_ln.reshape(N_EVEN, 1, D_CONF),
        },
        "zero_h": jnp.zeros((1, nb_p, D_SSD, SSD_STATE), F32),
        "zero_sc": jnp.zeros((1, nb_p, SC_ROWS, SSD_CONV_DIM), F32),
        "zero_dw": jnp.zeros((1, nb_p, DW_ROWS, D_CONF), F32),
    }

    n_c = nb_p + nb_s
    c_rows = -(-n_c // SUBLANES) * SUBLANES
    c_all = jnp.pad(jnp.concatenate([c_prompt, c_sample], axis=0), ((0, c_rows - n_c), (0, 0)))
    mod_all = _ada(c_all, w_ada, b_ada).reshape(DEPTH, c_rows, N_SUB, 3, d)
    mods_p, mods_s = [], []
    for l in range(DEPTH):
        mp, ms = [], []
        for s in range(N_SUB):
            mp.append(_Mod(mod_all[l, :nb_p, s][:, :, None, :], seq_p, False))
            rows = jnp.repeat(mod_all[l, nb_p:n_c, s], seq_s, axis=0)
            ms.append(_Mod(jnp.transpose(rows, (1, 0, 2)), seq_s, True))
        mods_p.append(mp)
        mods_s.append(ms)

    out_p = _trunk(x_prompt.reshape(nb_p * seq_p, d), nb_p, seq_p, mods_p, W, (1024, 1024, 1024, 512))

    n_phys = cache_k.shape[1]
    ck = cache_k.reshape(N_ODD, n_phys, PAGE_SIZE * ATTN_HEADS, ATTN_HEAD_DIM)
    cv = cache_v.reshape(N_ODD, n_phys, PAGE_SIZE * ATTN_HEADS, ATTN_HEAD_DIM)
    states = (
        state_ssm.reshape(N_EVEN, nb_s, D_SSD, SSD_STATE),
        jnp.pad(state_ssd_conv, ((0, 0), (0, 0), (SC_ROWS - (SSD_CONV - 1), 0), (0, 0))),
        jnp.pad(state_dwconv, ((0, 0), (0, 0), (DW_ROWS - (CONF_KERNEL - 1), 0), (0, 0))),
    )
    m_s = nb_s * seq_s
    out_s = _trunk(x_sample.reshape(m_s, d), nb_s, seq_s, mods_s, W, (m_s, m_s, m_s, m_s),
                   cache=(ck, cv, page_table.reshape(-1)), states=states)

    y_p, k_p, v_p, ssm_p, sc_p, dw_p = out_p
    y_s, k_s, v_s, ssm_s, sc_s, dw_s = out_s
    return (y_p, y_s, k_p, v_p, k_s, v_s, ssm_p, ssm_s, sc_p, sc_s, dw_p, dw_s)
```

```python
import functools

import jax
import jax.numpy as jnp
from jax import lax
from jax.experimental import pallas as pl
from jax.experimental.pallas import tpu as pltpu

F32 = jnp.float32
BF16 = jnp.bfloat16

D_MODEL = 2048
DEPTH = 4
N_EVEN = (DEPTH + 1) // 2
N_ODD = DEPTH // 2
N_SUB = 3
MACARON = 0.5
D_FF = 5632
EPS = 1e-6
D_SSD = 2048
SSD_HEAD_DIM = 64
SSD_HEADS = D_SSD // SSD_HEAD_DIM
SSD_GROUPS = 4
SSD_STATE = 128
SSD_CONV = 4
SSD_CHUNK = 128
SSD_BC = SSD_GROUPS * SSD_STATE
SSD_CONV_DIM = D_SSD + 2 * SSD_BC
D_CONF = 2048
CONF_KERNEL = 31
ATTN_HEADS = 16
ATTN_HEAD_DIM = D_MODEL // ATTN_HEADS
MOBA_BLOCK = 256
MOBA_TOPK = 3
PAGE_SIZE = 128

LANES = 128
SUBLANES = 8
VMEM_LIMIT = 56 << 20
IN_Z, IN_A, IN_G, IN_XBC, IN_DT = 0, 2048, 4096, 6144, 9216
IN_PAD = 9728
DW_ROWS = 32
SC_ROWS = 8
NEG_INF = float("-inf")


def _cparams(sem):
    return pltpu.CompilerParams(dimension_semantics=sem, vmem_limit_bytes=VMEM_LIMIT)


def _dot(a, b):
    return jnp.dot(a, b, preferred_element_type=F32)


def _dot_nt(a, b):
    return lax.dot_general(a, b, (((1,), (1,)), ((), ())), preferred_element_type=F32)


def _silu(x):
    return x * jax.nn.sigmoid(x)


def _softplus(x):
    return jnp.maximum(x, 0.0) + jnp.log1p(jnp.exp(-jnp.abs(x)))


def _adaln(x, g, shift, scale):
    ms = jnp.mean(x * x, axis=-1, keepdims=True)
    y = x * lax.rsqrt(ms + EPS) * g
    return y * (1.0 + scale) + shift


def _split_bf16(x):
    hi = x.astype(BF16)
    lo = (x - hi.astype(F32)).astype(BF16)
    return hi, lo


def _ada_kernel(c_ref, w_ref, b_ref, o_ref):
    cs = _silu(c_ref[...]).astype(BF16)
    o_ref[...] = _dot(cs, w_ref[...].astype(BF16)) + b_ref[...]


def _ada(c_all, w_ada, b_ada):
    rows = c_all.shape[0]
    n = w_ada.shape[-1]
    tn = 1024
    return pl.pallas_call(
        _ada_kernel,
        grid=(DEPTH, n // tn),
        in_specs=[
            pl.BlockSpec((rows, D_MODEL), lambda l, j: (0, 0)),
            pl.BlockSpec((None, D_MODEL, tn), lambda l, j: (l, 0, j)),
            pl.BlockSpec((None, 1, tn), lambda l, j: (l, 0, j)),
        ],
        out_specs=pl.BlockSpec((None, rows, tn), lambda l, j: (l, 0, j)),
        out_shape=jax.ShapeDtypeStruct((DEPTH, rows, n), F32),
        compiler_params=_cparams(("parallel", "parallel")),
        name="ada_mod",
    )(c_all, w_ada, b_ada.reshape(DEPTH, 1, n))


class _Mod:
    def __init__(self, arr, rows_per_seq, per_row):
        self.arr = arr
        self.rows_per_seq = rows_per_seq
        self.per_row = per_row

    def spec(self, tm, width, tiled):
        if self.per_row:
            return pl.BlockSpec((3, tm, width), lambda m, n: (0, m, n if tiled else 0))
        bps = self.rows_per_seq // tm
        return pl.BlockSpec((None, 3, 1, width), lambda m, n: (m // bps, 0, 0, n if tiled else 0))


def _ln_mm_kernel(x_ref, g_ref, mod_ref, w_ref, *rest, n_out, tiles_per_out, n_alias):
    o_refs, h_ref = rest[n_alias:n_alias + n_out], rest[n_alias + n_out]
    j = pl.program_id(1)

    @pl.when(j == 0)
    def _():
        h_ref[...] = _adaln(x_ref[...], g_ref[...], mod_ref[0], mod_ref[1]).astype(BF16)

    if n_out == 1:
        o_refs[0][...] = _dot(h_ref[...], w_ref[...])
    for i in range(n_out if n_out > 1 else 0):
        @pl.when(j // tiles_per_out == i)
        def _(i=i):
            o_refs[i][...] = _dot(h_ref[...], w_ref[...])


def _ln_mm(x, g_norm3, gi, mod, w, wi, tm, tn, n_out=1, slab=None, stacked_prev=()):
    m_rows = x.shape[0]
    n = w.shape[-1]
    tpo = n // n_out // tn
    col = lambda j, i: jnp.clip(j - i * tpo, 0, tpo - 1)
    out_specs = [pl.BlockSpec((tm, tn), lambda m, j: (m, col(j, 0)))]
    out_shape = [jax.ShapeDtypeStruct((m_rows, n // n_out), F32)]
    for i in range(1, n_out):
        if slab is None:
            out_specs.append(pl.BlockSpec((tm, tn), lambda m, j, i=i: (m, col(j, i))))
            out_shape.append(jax.ShapeDtypeStruct((m_rows, n // n_out), F32))
        else:
            out_specs.append(pl.BlockSpec((None, tm, tn), lambda m, j, i=i: (slab[0], m, col(j, i))))
            out_shape.append(jax.ShapeDtypeStruct((slab[1], m_rows, n // n_out), F32))
    n_alias = len(stacked_prev)
    outs = pl.pallas_call(
        functools.partial(_ln_mm_kernel, n_out=n_out, tiles_per_out=tpo, n_alias=n_alias),
        grid=(m_rows // tm, n // tn),
        in_specs=[
            pl.BlockSpec((tm, D_MODEL), lambda m, j: (m, 0)),
            pl.BlockSpec((None, 1, D_MODEL), lambda m, j: (gi, 0, 0)),
            mod.spec(tm, D_MODEL, False),
            pl.BlockSpec((None, D_MODEL, tn), lambda m, j: (wi, 0, j)),
        ] + [pl.BlockSpec(memory_space=pl.ANY)] * n_alias,
        out_specs=out_specs,
        out_shape=out_shape,
        input_output_aliases={4 + i: 1 + i for i in range(n_alias)},
        scratch_shapes=[pltpu.VMEM((tm, D_MODEL), BF16)],
        compiler_params=_cparams(("parallel", "arbitrary")),
        name="ln_proj",
    )(x, g_norm3, mod.arr, w, *stacked_prev)
    return outs[0] if n_out == 1 else outs


def _ffn_kernel(x_ref, g_ref, mod_ref, wg_ref, wu_ref, wd_ref, gf_ref, o_ref, h_ref, *, nf, final_norm):
    f = pl.program_id(1)

    @pl.when(f == 0)
    def _():
        h_ref[...] = _adaln(x_ref[...], g_ref[...], mod_ref[0], mod_ref[1]).astype(BF16)
        o_ref[...] = jnp.zeros_like(o_ref)

    h = h_ref[...]
    a = (_silu(_dot(h, wg_ref[...])) * _dot(h, wu_ref[...])).astype(BF16)
    o_ref[...] += _dot(a, wd_ref[...])

    @pl.when(f == nf - 1)
    def _():
        y = x_ref[...] + (MACARON * (1.0 + mod_ref[2])) * o_ref[...]
        if final_norm:
            ms = jnp.mean(y * y, axis=-1, keepdims=True)
            y = y * lax.rsqrt(ms + EPS) * gf_ref[...]
        o_ref[...] = y


def _ffn(x, g_norm3, gi, mod, wg, wu, wd, l, s, tm, tf, g_final, final_norm=False):
    m_rows = x.shape[0]
    nf = D_FF // tf
    return pl.pallas_call(
        functools.partial(_ffn_kernel, nf=nf, final_norm=final_norm),
        grid=(m_rows // tm, nf),
        in_specs=[
            pl.BlockSpec((tm, D_MODEL), lambda m, f: (m, 0)),
            pl.BlockSpec((None, 1, D_MODEL), lambda m, f: (gi, 0, 0)),
            mod.spec(tm, D_MODEL, False),
            pl.BlockSpec((None, None, D_MODEL, tf), lambda m, f: (l, s, 0, f)),
            pl.BlockSpec((None, None, D_MODEL, tf), lambda m, f: (l, s, 0, f)),
            pl.BlockSpec((None, None, tf, D_MODEL), lambda m, f: (l, s, f, 0)),
            pl.BlockSpec((1, D_MODEL), lambda m, f: (0, 0)),
        ],
        out_specs=pl.BlockSpec((tm, D_MODEL), lambda m, f: (m, 0)),
        out_shape=jax.ShapeDtypeStruct((m_rows, D_MODEL), F32),
        scratch_shapes=[pltpu.VMEM((tm, D_MODEL), BF16)],
        compiler_params=_cparams(("parallel", "arbitrary")),
        name="ffn",
    )(x, g_norm3, mod.arr, wg, wu, wd, g_final)


def _mm_res_kernel(*refs, na):
    a_refs = refs[:na]
    w_refs = refs[na:2 * na]
    x_ref, mod_ref, o_ref = refs[2 * na:]
    acc = _dot(a_refs[0][...].astype(BF16), w_refs[0][...])
    for i in range(1, na):
        acc = acc + _dot(a_refs[i][...].astype(BF16), w_refs[i][...])
    o_ref[...] = x_ref[...] + (1.0 + mod_ref[2]) * acc


def _mm_res(a_list, w, wi, x, mod, tm, tn):
    m_rows = x.shape[0]
    na = len(a_list)
    ka = a_list[0].shape[1]
    in_specs = [pl.BlockSpec((tm, ka), lambda m, j: (m, 0)) for _ in a_list]
    in_specs += [pl.BlockSpec((None, None, ka, tn), lambda m, j, i=i: (wi, i, 0, j)) for i in range(na)]
    in_specs += [pl.BlockSpec((tm, tn), lambda m, j: (m, j)), mod.spec(tm, tn, True)]
    return pl.pallas_call(
        functools.partial(_mm_res_kernel, na=na),
        grid=(m_rows // tm, D_MODEL // tn),
        in_specs=in_specs,
        out_specs=pl.BlockSpec((tm, tn), lambda m, j: (m, j)),
        out_shape=jax.ShapeDtypeStruct((m_rows, D_MODEL), F32),
        compiler_params=_cparams(("parallel", "arbitrary")),
        name="mix_out",
    )(*a_list, *([w] * na), x, mod.arr)


def _cumsum_rows(v):
    rows = v.shape[0]
    row = lax.broadcasted_iota(jnp.int32, v.shape, 0)
    s = 1
    while s < rows:
        v = v + jnp.where(row >= s, pltpu.roll(v, s, axis=0), 0.0)
        s *= 2
    return v


def _expand_heads(v, e, passes):
    out = None
    r = v
    for _ in range(passes):
        hi = r.astype(BF16)
        t = _dot(hi, e)
        out = t if out is None else out + t
        r = r - hi.astype(F32)
    return out


def _ssd_kernel(z_ref, xbc_ref, dt_ref, wc_ref, bc_ref, dtb_ref, alog_ref, dsk_ref, gs_ref, e_ref,
                h0_ref, sc0_ref, y_ref, hn_ref, scn_ref, ht_ref, ext_ref, yacc_ref, *, nc, lv_last):
    q = SSD_CHUNK
    c = pl.program_id(1)

    @pl.when(c == 0)
    def _():
        ht_ref[...] = h0_ref[...].T
        ext_ref[0:SC_ROWS, :] = sc0_ref[...]

    ext_ref[SC_ROWS:SC_ROWS + q, :] = xbc_ref[...]
    base = SC_ROWS - (SSD_CONV - 1)
    acc = bc_ref[...] + wc_ref[0:1, :] * ext_ref[base:base + q, :]
    for k in range(1, SSD_CONV):
        acc = acc + wc_ref[k:k + 1, :] * ext_ref[base + k:base + k + q, :]
    xbc = _silu(acc)
    xs = xbc[:, :D_SSD]
    bm = xbc[:, D_SSD:D_SSD + SSD_BC]
    cm = xbc[:, D_SSD + SSD_BC:]

    dt = _softplus(dt_ref[...] + dtb_ref[...])
    if lv_last < q:
        row = lax.broadcasted_iota(jnp.int32, (q, LANES), 0)
        dt = jnp.where(row < jnp.where(c == nc - 1, lv_last, q), dt, 0.0)
    a_neg = -jnp.exp(alog_ref[...])
    a_cum = _cumsum_rows(dt * a_neg)
    a_cum_t = a_cum.T
    a_last = a_cum[q - 1:q, :]
    e = e_ref[...]
    xdt = xs * _expand_heads(dt, e, 2)
    xw = (xdt * _expand_heads(jnp.exp(a_last - a_cum), e, 2)).astype(BF16)
    expa_e = _expand_heads(jnp.exp(a_cum), e, 2)
    cd_e = _expand_heads(jnp.broadcast_to(jnp.exp(a_last), (SUBLANES, LANES)), e, 3)[0:1, :]
    qi = lax.broadcasted_iota(jnp.int32, (q, q), 0)
    si = lax.broadcasted_iota(jnp.int32, (q, q), 1)
    causal = qi >= si
    lane = lax.broadcasted_iota(jnp.int32, (q, LANES), 1)
    gw = D_SSD // SSD_GROUPS
    for g in range(SSD_GROUPS):
        bg = bm[:, g * SSD_STATE:(g + 1) * SSD_STATE]
        cg = cm[:, g * SSD_STATE:(g + 1) * SSD_STATE].astype(BF16)
        cb = _dot_nt(cg, bg.astype(BF16))
        gsl = slice(g * gw, (g + 1) * gw)
        htg = ht_ref[:, gsl]
        y_inter = _dot(cg, htg.astype(BF16)) * expa_e[:, gsl]
        ht_ref[:, gsl] = cd_e[:, gsl] * htg + _dot(bg.T.astype(BF16), xw[:, gsl])
        for pr in range(gw // LANES):
            lo = g * gw + pr * LANES
            xp = xdt[:, lo:lo + LANES]
            ypair = y_inter[:, pr * LANES:(pr + 1) * LANES]
            for half in range(2):
                h = lo // SSD_HEAD_DIM + half
                seg = a_cum[:, h:h + 1] - a_cum_t[h:h + 1, :]
                dec = jnp.exp(jnp.where(causal, seg, NEG_INF))
                mh = (cb * dec).astype(BF16)
                in_half = (lane >= SSD_HEAD_DIM) if half else (lane < SSD_HEAD_DIM)
                ypair = ypair + _dot(mh, jnp.where(in_half, xp, 0.0).astype(BF16))
            yacc_ref[:, lo:lo + LANES] = ypair
    y = (yacc_ref[...] + dsk_ref[...] * xs) * _silu(z_ref[...])
    for g in range(SSD_GROUPS):
        gsl = slice(g * gw, (g + 1) * gw)
        yg = y[:, gsl]
        ms = jnp.mean(yg * yg, axis=-1, keepdims=True)
        y_ref[:, gsl] = (yg * lax.rsqrt(ms + EPS) * gs_ref[:, gsl]).astype(y_ref.dtype)

    @pl.when(c == nc - 1)
    def _():
        hn_ref[...] = ht_ref[...].T
        scn_ref[...] = ext_ref[lv_last:lv_last + SC_ROWS, :]

    @pl.when(c < nc - 1)
    def _():
        ext_ref[0:SC_ROWS, :] = ext_ref[q:q + SC_ROWS, :]


def _ssd(proj, nb, nc, lv_last, e, p, e_mat, h0, h0_e, sc0, sc0_e):
    q = SSD_CHUNK
    rows = proj.shape[0]
    vec = lambda width: pl.BlockSpec((None, 1, width), lambda b, c: (e, 0, 0))
    return pl.pallas_call(
        functools.partial(_ssd_kernel, nc=nc, lv_last=lv_last),
        grid=(nb, nc),
        in_specs=[
            pl.BlockSpec((q, D_SSD), lambda b, c: (b * nc + c, IN_Z // D_SSD)),
            pl.BlockSpec((q, SSD_CONV_DIM), lambda b, c: (b * nc + c, IN_XBC // SSD_CONV_DIM)),
            pl.BlockSpec((q, LANES), lambda b, c: (b * nc + c, IN_DT // LANES)),
            pl.BlockSpec((None, SSD_CONV, SSD_CONV_DIM), lambda b, c: (e, 0, 0)),
            vec(SSD_CONV_DIM), vec(LANES), vec(LANES), vec(D_SSD), vec(D_SSD),
            pl.BlockSpec((LANES, D_SSD), lambda b, c: (0, 0)),
            pl.BlockSpec((None, None, D_SSD, SSD_STATE), lambda b, c: (h0_e, b, 0, 0)),
            pl.BlockSpec((None, None, SC_ROWS, SSD_CONV_DIM), lambda b, c: (sc0_e, b, 0, 0)),
        ],
        out_specs=[
            pl.BlockSpec((q, D_SSD), lambda b, c: (b * nc + c, 0)),
            pl.BlockSpec((None, D_SSD, SSD_STATE), lambda b, c: (b, 0, 0)),
            pl.BlockSpec((None, SC_ROWS, SSD_CONV_DIM), lambda b, c: (b, 0, 0)),
        ],
        out_shape=[
            jax.ShapeDtypeStruct((rows, D_SSD), BF16),
            jax.ShapeDtypeStruct((nb, D_SSD, SSD_STATE), F32),
            jax.ShapeDtypeStruct((nb, SC_ROWS, SSD_CONV_DIM), F32),
        ],
        scratch_shapes=[
            pltpu.VMEM((SSD_STATE, D_SSD), F32),
            pltpu.VMEM((SC_ROWS + q, SSD_CONV_DIM), F32),
            pltpu.VMEM((q, D_SSD), F32),
        ],
        compiler_params=_cparams(("parallel", "arbitrary")),
        name="ssd_scan",
    )(proj, proj, proj, p["w_conv"], p["b_conv"], p["dt_bias"], p["a_log"], p["d_skip"], p["g_ssd"],
      e_mat, h0, sc0)


def _conf_kernel(a_ref, gt_ref, wdw_ref, bdw_ref, lng_ref, lnb_ref, dw0_ref, u_ref, dwn_ref,
                 ext_ref, cv_ref, sh_ref, *, nc, lv_last):
    q = SSD_CHUNK
    c = pl.program_id(1)

    @pl.when(c == 0)
    def _():
        ext_ref[0:DW_ROWS, :] = dw0_ref[...]

    ext_ref[DW_ROWS:DW_ROWS + q, :] = a_ref[...] * jax.nn.sigmoid(gt_ref[...])
    base = DW_ROWS - (CONF_KERNEL - 1)
    sh_rows = sh_ref.shape[1]
    for r in range(1, SUBLANES):
        sh_ref[r - 1] = ext_ref[r:r + sh_rows, :]
    ct = 256
    for t in range(D_CONF // ct):
        cs = slice(t * ct, (t + 1) * ct)
        acc = bdw_ref[:, cs]
        for k in range(CONF_KERNEL):
            r, al = (base + k) % SUBLANES, (base + k) // SUBLANES * SUBLANES
            win = ext_ref[al:al + q, cs] if r == 0 else sh_ref[r - 1, al:al + q, cs]
            acc = acc + wdw_ref[k:k + 1, cs] * win
        cv_ref[:, cs] = acc
    v = cv_ref[...]
    mu = jnp.mean(v, axis=-1, keepdims=True)
    xc = v - mu
    var = jnp.mean(xc * xc, axis=-1, keepdims=True)
    yn = xc * lax.rsqrt(var + EPS) * lng_ref[...] + lnb_ref[...]
    u_ref[...] = _silu(yn).astype(u_ref.dtype)

    @pl.when(c == nc - 1)
    def _():
        dwn_ref[...] = ext_ref[lv_last:lv_last + DW_ROWS, :]

    @pl.when(c < nc - 1)
    def _():
        ext_ref[0:DW_ROWS, :] = ext_ref[q:q + DW_ROWS, :]


def _conf(proj, nb, nc, lv_last, e, p, dw0, dw0_e):
    q = SSD_CHUNK
    rows = proj.shape[0]
    vec = lambda: pl.BlockSpec((None, 1, D_CONF), lambda b, c: (e, 0, 0))
    return pl.pallas_call(
        functools.partial(_conf_kernel, nc=nc, lv_last=lv_last),
        grid=(nb, nc),
        in_specs=[
            pl.BlockSpec((q, D_CONF), lambda b, c: (b * nc + c, IN_A // D_CONF)),
            pl.BlockSpec((q, D_CONF), lambda b, c: (b * nc + c, IN_G // D_CONF)),
            pl.BlockSpec((None, DW_ROWS, D_CONF), lambda b, c: (e, 0, 0)),
            vec(), vec(), vec(),
            pl.BlockSpec((None, None, DW_ROWS, D_CONF), lambda b, c: (dw0_e, b, 0, 0)),
        ],
        out_specs=[
            pl.BlockSpec((q, D_CONF), lambda b, c: (b * nc + c, 0)),
            pl.BlockSpec((None, DW_ROWS, D_CONF), lambda b, c: (b, 0, 0)),
        ],
        out_shape=[
            jax.ShapeDtypeStruct((rows, D_CONF), BF16),
            jax.ShapeDtypeStruct((nb, DW_ROWS, D_CONF), F32),
        ],
        scratch_shapes=[
            pltpu.VMEM((DW_ROWS + q, D_CONF), F32),
            pltpu.VMEM((q, D_CONF), F32),
            pltpu.VMEM((SUBLANES - 1, DW_ROWS + q - SUBLANES, D_CONF), F32),
        ],
        compiler_params=_cparams(("parallel", "arbitrary")),
        name="conf_conv",
    )(proj, proj, p["w_dw"], p["b_dw"], p["ln_g"], p["ln_b"], dw0)


def _moba_prompt_kernel(q_ref, k_ref, v_ref, o_ref, s_ref, *, seq):
    blk = MOBA_BLOCK
    nb = seq // blk
    scale = ATTN_HEAD_DIM ** -0.5
    q = q_ref[...]
    k = k_ref[...]
    km = jnp.mean(k.reshape(nb, blk, ATTN_HEAD_DIM), axis=1)
    qh, ql = _split_bf16(q)
    kh, kl = _split_bf16(km)
    sc_t = _dot_nt(kh, qh) + _dot_nt(kh, ql) + _dot_nt(kl, qh)
    qs = (q * scale).astype(BF16)
    kb = k.astype(BF16)
    v_t = v_ref[...].T.astype(BF16)
    sub = lax.broadcasted_iota(jnp.int32, (nb, blk), 0)
    kidx = lax.broadcasted_iota(jnp.int32, (blk, blk), 0)
    qidx = lax.broadcasted_iota(jnp.int32, (blk, blk), 1)
    for j in range(nb):
        sc = sc_t[:, j * blk:(j + 1) * blk]
        rank = jnp.zeros((nb, blk), F32)
        for i in range(j):
            r = sc[i:i + 1, :]
            beats = jnp.where(r > sc, 1.0, jnp.where(r == sc, jnp.where(sub > i, 1.0, 0.0), 0.0))
            rank = rank + beats
        bias = jnp.where(rank < MOBA_TOPK, jnp.where(sub < j, 0.0, NEG_INF), NEG_INF)
        qj = qs[j * blk:(j + 1) * blk, :]
        for i in range(j + 1):
            s = _dot_nt(kb[i * blk:(i + 1) * blk, :], qj)
            if i < j:
                s = s + bias[i:i + 1, :]
            else:
                s = jnp.where(kidx <= qidx, s, NEG_INF)
            s_ref[i * blk:(i + 1) * blk, :] = s
        nk = (j + 1) * blk
        s_all = s_ref[0:nk, :]
        m = jnp.max(s_all, axis=0, keepdims=True)
        p = jnp.exp(s_all - m)
        l = jnp.sum(p, axis=0, keepdims=True)
        o_t = _dot(v_t[:, 0:nk], p.astype(BF16)) / l
        o_ref[j * blk:(j + 1) * blk, :] = o_t.T.astype(o_ref.dtype)


def _moba_prompt(q, k, v, o, nb, seq):
    spec = pl.BlockSpec((seq, ATTN_HEAD_DIM), lambda b, h: (b, h))
    kv_spec = pl.BlockSpec((None, seq, ATTN_HEAD_DIM), lambda b, h: (o, b, h))
    return pl.pallas_call(
        functools.partial(_moba_prompt_kernel, seq=seq),
        grid=(nb, ATTN_HEADS),
        in_specs=[spec, kv_spec, kv_spec],
        out_specs=spec,
        out_shape=jax.ShapeDtypeStruct((nb * seq, D_MODEL), BF16),
        scratch_shapes=[pltpu.VMEM((seq, MOBA_BLOCK), F32)],
        compiler_params=_cparams(("parallel", "parallel")),
        name="moba_prompt",
    )(q, k, v)


PAGES_PER_STEP = 8
BLOCKS_PER_STEP = PAGES_PER_STEP * PAGE_SIZE // MOBA_BLOCK
HEAD_HALVES = 2
HEADS_LOW = ATTN_HEADS // HEAD_HALVES
FOLD = HEADS_LOW * ATTN_HEAD_DIM
PKEYS = HEAD_HALVES * MOBA_BLOCK


def _load_page_folded(ref):
    return jnp.concatenate(
        [ref[pl.ds(low, PAGE_SIZE * HEAD_HALVES, stride=HEADS_LOW), :] for low in range(HEADS_LOW)], axis=1)


def _fold_lanes(x, nq):
    rows = x.shape[0]
    first = lax.broadcasted_iota(jnp.int32, (rows, FOLD), 0) < HEADS_LOW * nq
    return jnp.where(first, x[:, :FOLD], x[:, FOLD:])


def _own_half(shape, nq):
    col = lax.broadcasted_iota(jnp.int32, shape, 1)
    row = lax.broadcasted_iota(jnp.int32, shape, 0)
    return (col % HEAD_HALVES) == (row // (HEADS_LOW * nq))


def _block_diag_queries(q8):
    nq = q8.shape[0]
    rows = ATTN_HEADS * nq
    qrep = jnp.concatenate([q8] * ATTN_HEADS, axis=0)
    rh = lax.broadcasted_iota(jnp.int32, (rows, D_MODEL), 0) // nq
    ch = lax.broadcasted_iota(jnp.int32, (rows, D_MODEL), 1) // ATTN_HEAD_DIM
    return jnp.where(rh == ch, qrep, 0.0), rh == ch


def _sample_scores_kernel(pt_ref, q_ref, *refs, nsteps):
    k_refs = refs[:PAGES_PER_STEP]
    s_ref, bs_ref, bm_ref = refs[PAGES_PER_STEP:]
    st = pl.program_id(1)
    scale = ATTN_HEAD_DIM ** -0.5
    nq = q_ref.shape[0]
    qbd, _ = _block_diag_queries(q_ref[...])
    qf = _fold_lanes(qbd, nq).astype(BF16)
    own = _own_half((ATTN_HEADS * nq, PKEYS), nq)

    @pl.when(st == 0)
    def _():
        bs_ref[...] = jnp.zeros_like(bs_ref)
        bm_ref[...] = jnp.full(bm_ref.shape, NEG_INF, F32)

    bs = bs_ref[...]
    bm = bm_ref[...]
    lane = lax.broadcasted_iota(jnp.int32, bs.shape, 1)
    ppb = MOBA_BLOCK // PAGE_SIZE
    for j in range(BLOCKS_PER_STEP):
        kblk = jnp.concatenate([_load_page_folded(k_refs[ppb * j + r]) for r in range(ppb)], axis=0).astype(BF16)
        sj = _dot_nt(qf, kblk) * scale
        sm = jnp.where(own, sj, NEG_INF)
        s_ref[j] = sm
        bmean = jnp.sum(jnp.where(own, sj, 0.0), axis=-1, keepdims=True) * (1.0 / MOBA_BLOCK)
        here = lane == st * BLOCKS_PER_STEP + j
        bs = jnp.where(here, bmean, bs)
        bm = jnp.where(here, jnp.max(sm, axis=-1, keepdims=True), bm)
    bs_ref[...] = bs
    bm_ref[...] = bm


def _top_blocks(bs, nvalid):
    lane = lax.broadcasted_iota(jnp.int32, bs.shape, 1).astype(F32)
    cur = jnp.where(lane < nvalid, bs, NEG_INF)
    sel = jnp.zeros(bs.shape, F32)
    for _ in range(MOBA_TOPK):
        m = jnp.max(cur, axis=-1, keepdims=True)
        idx = jnp.min(jnp.where(cur == m, lane, 1e9), axis=-1, keepdims=True)
        pick = jnp.where(lane == idx, jnp.where(m > NEG_INF, 1.0, 0.0), 0.0)
        sel = jnp.maximum(sel, pick)
        cur = jnp.where(pick > 0.5, NEG_INF, cur)
    return sel


def _sample_attn_kernel(pt_ref, s_ref, bs_ref, bm_ref, q_ref, kn_ref, vn_ref, *refs, nsteps, nblocks):
    v_refs = refs[:PAGES_PER_STEP]
    o_ref, p_ref, acc_ref, l_ref = refs[PAGES_PER_STEP:]
    st = pl.program_id(1)
    nq = q_ref.shape[0]
    rows = ATTN_HEADS * nq
    scale = ATTN_HEAD_DIM ** -0.5

    @pl.when(st == 0)
    def _():
        sel = _top_blocks(bs_ref[...], nblocks)
        lane = lax.broadcasted_iota(jnp.int32, (rows, LANES), 1)
        qbd, _ = _block_diag_queries(q_ref[...])
        pad = jnp.zeros((LANES - nq, D_MODEL), F32)
        kn = jnp.concatenate([kn_ref[...], pad], axis=0).astype(BF16)
        vn = jnp.concatenate([vn_ref[...], pad], axis=0).astype(BF16)
        s_own = _dot_nt(qbd.astype(BF16), kn) * scale
        qpos = lax.broadcasted_iota(jnp.int32, (rows, LANES), 0) % nq
        s_own = jnp.where(lane <= qpos, s_own, NEG_INF)

        def sel_col(n):
            return jnp.max(jnp.where(lane == n, sel, 0.0), axis=-1, keepdims=True) > 0.5

        m_past = jnp.max(jnp.where(sel > 0.5, bm_ref[...], NEG_INF), axis=-1, keepdims=True)
        m = jnp.maximum(m_past, jnp.max(s_own, axis=-1, keepdims=True))
        p_own = jnp.exp(s_own - m)

        def p_body(n, l):
            pn = jnp.where(sel_col(n), jnp.exp(s_ref[n] - m), 0.0)
            p_ref[n] = pn.astype(BF16)
            return l + jnp.sum(pn, axis=-1, keepdims=True)

        l = lax.fori_loop(0, nblocks, p_body, jnp.sum(p_own, axis=-1, keepdims=True))
        l_ref[...] = jnp.broadcast_to(l, l_ref.shape)
        acc_ref[...] = _fold_lanes(_dot(p_own.astype(BF16), vn), nq)

    ppb = MOBA_BLOCK // PAGE_SIZE
    acc = acc_ref[...]
    for j in range(BLOCKS_PER_STEP):
        vblk = jnp.concatenate([_load_page_folded(v_refs[ppb * j + r]) for r in range(ppb)], axis=0).astype(BF16)
        acc = acc + _dot(p_ref[st * BLOCKS_PER_STEP + j], vblk)
    acc_ref[...] = acc

    @pl.when(st == nsteps - 1)
    def _():
        rl = (lax.broadcasted_iota(jnp.int32, (rows, FOLD), 0) // nq) % HEADS_LOW
        cl = lax.broadcasted_iota(jnp.int32, (rows, FOLD), 1) // ATTN_HEAD_DIM
        o_full = jnp.where(rl == cl, acc_ref[...] / l_ref[:, 0:1], 0.0).reshape(ATTN_HEADS, nq, FOLD)
        o_ref[...] = jnp.concatenate(
            [jnp.sum(o_full[half * HEADS_LOW:(half + 1) * HEADS_LOW], axis=0) for half in range(HEAD_HALVES)],
            axis=1)


def _page_specs(o, n_pages_seq):
    specs = []
    for r in range(PAGES_PER_STEP):
        specs.append(pl.BlockSpec(
            (None, None, PAGE_SIZE * ATTN_HEADS, ATTN_HEAD_DIM),
            lambda b, st, pt, r=r: (o, pt[b * n_pages_seq + st * PAGES_PER_STEP + r], 0, 0)))
    return specs


def _moba_sample(q, k_new, v_new, cache_k, cache_v, pt_flat, o, nseq, nq):
    n_pages_seq = pt_flat.shape[0] // nseq
    nblocks = n_pages_seq * PAGE_SIZE // MOBA_BLOCK
    nsteps = n_pages_seq // PAGES_PER_STEP
    rows = ATTN_HEADS * nq
    s_all, bscore, bmax = pl.pallas_call(
        functools.partial(_sample_scores_kernel, nsteps=nsteps),
        grid_spec=pltpu.PrefetchScalarGridSpec(
            num_scalar_prefetch=1,
            grid=(nseq, nsteps),
            in_specs=[pl.BlockSpec((nq, D_MODEL), lambda b, st, pt: (b, 0))] + _page_specs(o, n_pages_seq),
            out_specs=[
                pl.BlockSpec((None, BLOCKS_PER_STEP, rows, PKEYS), lambda b, st, pt: (b, st, 0, 0)),
                pl.BlockSpec((None, rows, LANES), lambda b, st, pt: (b, 0, 0)),
                pl.BlockSpec((None, rows, LANES), lambda b, st, pt: (b, 0, 0)),
            ],
        ),
        out_shape=[
            jax.ShapeDtypeStruct((nseq, nblocks, rows, PKEYS), F32),
            jax.ShapeDtypeStruct((nseq, rows, LANES), F32),
            jax.ShapeDtypeStruct((nseq, rows, LANES), F32),
        ],
        compiler_params=_cparams(("parallel", "arbitrary")),
        name="moba_sample_scores",
    )(pt_flat, q, *([cache_k] * PAGES_PER_STEP))
    return pl.pallas_call(
        functools.partial(_sample_attn_kernel, nsteps=nsteps, nblocks=nblocks),
        grid_spec=pltpu.PrefetchScalarGridSpec(
            num_scalar_prefetch=1,
            grid=(nseq, nsteps),
            in_specs=[
                pl.BlockSpec((None, nblocks, rows, PKEYS), lambda b, st, pt: (b, 0, 0, 0),
                             pipeline_mode=pl.Buffered(1)),
                pl.BlockSpec((None, rows, LANES), lambda b, st, pt: (b, 0, 0)),
                pl.BlockSpec((None, rows, LANES), lambda b, st, pt: (b, 0, 0)),
                pl.BlockSpec((nq, D_MODEL), lambda b, st, pt: (b, 0)),
                pl.BlockSpec((None, nq, D_MODEL), lambda b, st, pt: (o, b, 0)),
                pl.BlockSpec((None, nq, D_MODEL), lambda b, st, pt: (o, b, 0)),
            ] + _page_specs(o, n_pages_seq),
            out_specs=pl.BlockSpec((nq, D_MODEL), lambda b, st, pt: (b, 0)),
            scratch_shapes=[
                pltpu.VMEM((nblocks, rows, PKEYS), BF16),
                pltpu.VMEM((rows, FOLD), F32),
                pltpu.VMEM((rows, LANES), F32),
            ],
        ),
        out_shape=jax.ShapeDtypeStruct((nseq * nq, D_MODEL), F32),
        compiler_params=_cparams(("parallel", "arbitrary")),
        name="moba_sample_attn",
    )(pt_flat, s_all, bscore, bmax, q, k_new, v_new, *([cache_v] * PAGES_PER_STEP))


def _trunk(x, nseq, seq, mods, W, tiles, cache=None, states=None):
    tm_proj, tm_ffn, tm_out, tm_qkv = tiles
    q = SSD_CHUNK
    seq_pad = -(-seq // q) * q
    nc = seq_pad // q
    lv_last = seq - (nc - 1) * q
    ssm, scs, dws = [], [], []
    kv_prev = ()
    ffn =functools.partial(_ffn, wg=W["wg"], wu=W["wu"], wd=W["wd"], tm=tm_ffn, tf=512, g_final=W["g_final"])
    for l in range(DEPTH):
        x = ffn(x, W["g_norm"], l * N_SUB, mods[l][0], l=l, s=0)
        if l % 2 == 0:
            e = l // 2
            proj = _ln_mm(x, W["g_norm"], l * N_SUB + 1, mods[l][1], W["w_in"], e, tm_proj, 512)
            if seq_pad != seq:
                proj = jnp.pad(proj.reshape(nseq, seq, IN_PAD), ((0, 0), (0, seq_pad - seq), (0, 0)))
                proj = proj.reshape(nseq * seq_pad, IN_PAD)
            if states is None:
                h0, h0_e, sc0, sc0_e, dw0, dw0_e = W["zero_h"], 0, W["zero_sc"], 0, W["zero_dw"], 0
            else:
                h0, sc0, dw0 = states
                h0_e = sc0_e = dw0_e = e
            y, hn, scn = _ssd(proj, nseq, nc, lv_last, e, W["ssd"], W["e_mat"], h0, h0_e, sc0, sc0_e)
            u, dwn = _conf(proj, nseq, nc, lv_last, e, W["conf"], dw0, dw0_e)
            if seq_pad != seq:
                y = y.reshape(nseq, seq_pad, D_SSD)[:, :seq].reshape(nseq * seq, D_SSD)
                u = u.reshape(nseq, seq_pad, D_CONF)[:, :seq].reshape(nseq * seq, D_CONF)
            ssm.append(hn.reshape(nseq, SSD_HEADS, SSD_HEAD_DIM, SSD_STATE))
            scs.append(scn[:, SC_ROWS - (SSD_CONV - 1):])
            dws.append(dwn[:, DW_ROWS - (CONF_KERNEL - 1):])
            x = _mm_res([y, u], W["w_out"], e, x, mods[l][1], tm_out, 512)
        else:
            o = l // 2
            qm, k_all, v_all = _ln_mm(x, W["g_norm"], l * N_SUB + 1, mods[l][1], W["w_qkv"], o, tm_qkv, 1024,
                                      n_out=3, slab=(o, N_ODD), stacked_prev=kv_prev)
            kv_prev = (k_all, v_all)
            if cache is None:
                att = _moba_prompt(qm, k_all, v_all, o, nseq, seq)
            else:
                cache_k, cache_v, pt_flat = cache
                att = _moba_sample(qm, k_all, v_all, cache_k, cache_v, pt_flat, o, nseq, seq)
            x = _mm_res([att], W["w_o"], o, x, mods[l][1], tm_out, 512)
        x = ffn(x, W["g_norm"], l * N_SUB + 2, mods[l][2], l=l, s=1, final_norm=(l == DEPTH - 1))
    kv_shape = (N_ODD, nseq, seq, ATTN_HEADS, ATTN_HEAD_DIM)
    return (x.reshape(nseq, seq, D_MODEL), kv_prev[0].reshape(kv_shape), kv_prev[1].reshape(kv_shape),
            jnp.stack(ssm), jnp.stack(scs), jnp.stack(dws))


def kernel(x_prompt, x_sample, cache_k, cache_v, page_table, state_ssm, state_ssd_conv, state_dwconv, c_prompt, c_sample, w_ada, b_ada, g_norm, w_ff_gate, w_ff_up, w_ff_down, w_in_even, w_conv_ssd, b_conv_ssd, dt_bias, a_log, d_skip, g_ssd_norm, w_dw, b_dw, g_conf_ln, b_conf_ln, w_out_even, w_qkv, w_o, g_final):
    nb_p, seq_p, _ = x_prompt.shape
    nb_s, seq_s, _ = x_sample.shape
    d = D_MODEL

    i1, i2, i3 = D_SSD, D_SSD + SSD_CONV_DIM, D_SSD + SSD_CONV_DIM + SSD_HEADS
    w_in = jnp.concatenate([
        w_in_even[..., :i1], w_in_even[..., i3:], w_in_even[..., i1:i2], w_in_even[..., i2:i3],
        jnp.zeros((N_EVEN, d, IN_PAD - IN_DT - SSD_HEADS), w_in_even.dtype)], axis=-1).astype(BF16)
    pad_h = lambda a: jnp.pad(a, ((0, 0), (0, LANES - SSD_HEADS))).reshape(N_EVEN, 1, LANES)
    lane_head = jnp.arange(D_SSD)[None, :] // SSD_HEAD_DIM
    W = {
        "g_norm": g_norm.reshape(DEPTH * N_SUB, 1, d),
        "wg": w_ff_gate.astype(BF16), "wu": w_ff_up.astype(BF16), "wd": w_ff_down.astype(BF16),
        "w_in": w_in,
        "w_out": w_out_even.astype(BF16).reshape(N_EVEN, 2, D_SSD, d),
        "w_qkv": w_qkv.astype(BF16),
        "w_o": w_o.astype(BF16).reshape(N_ODD, 1, d, d),
        "g_final": g_final.reshape(1, d),
        "e_mat": (jnp.arange(LANES)[:, None] == lane_head).astype(BF16),
        "ssd": {
            "w_conv": w_conv_ssd, "b_conv": b_conv_ssd.reshape(N_EVEN, 1, SSD_CONV_DIM),
            "dt_bias": pad_h(dt_bias), "a_log": pad_h(a_log),
            "d_skip": jnp.repeat(d_skip, SSD_HEAD_DIM, axis=-1).reshape(N_EVEN, 1, D_SSD),
            "g_ssd": g_ssd_norm.reshape(N_EVEN, 1, D_SSD),
        },
        "conf": {
            "w_dw": jnp.pad(w_dw, ((0, 0), (0, DW_ROWS - CONF_KERNEL), (0, 0))),
            "b_dw": b_dw.reshape(N_EVEN, 1, D_CONF),
            "ln_g": g_conf_ln.reshape(N_EVEN, 1, D_CONF), "ln_b": b_conf_ln.reshape(N_EVEN, 1, D_CONF),
        },
        "zero_h": jnp.zeros((1, nb_p, D_SSD, SSD_STATE), F32),
        "zero_sc": jnp.zeros((1, nb_p, SC_ROWS, SSD_CONV_DIM), F32),
        "zero_dw": jnp.zeros((1, nb_p, DW_ROWS, D_CONF), F32),
    }

    n_c = nb_p + nb_s
    c_rows = -(-n_c // SUBLANES) * SUBLANES
    c_all = jnp.pad(jnp.concatenate([c_prompt, c_sample], axis=0), ((0, c_rows - n_c), (0, 0)))
    mod_all = _ada(c_all, w_ada, b_ada).reshape(DEPTH, c_rows, N_SUB, 3, d)
    mods_p, mods_s = [], []
    for l in range(DEPTH):
        mp, ms = [], []
        for s in range(N_SUB):
            mp.append(_Mod(mod_all[l, :nb_p, s][:, :, None, :], seq_p, False))
            rows = jnp.repeat(mod_all[l, nb_p:n_c, s], seq_s, axis=0)
            ms.append(_Mod(jnp.transpose(rows, (1, 0, 2)), seq_s, True))
        mods_p.append(mp)
        mods_s.append(ms)

    out_p = _trunk(x_prompt.reshape(nb_p * seq_p, d), nb_p, seq_p, mods_p, W, (1024, 512, 1024, 512))

    n_phys = cache_k.shape[1]
    ck = cache_k.reshape(N_ODD, n_phys, PAGE_SIZE * ATTN_HEADS, ATTN_HEAD_DIM)
    cv = cache_v.reshape(N_ODD, n_phys, PAGE_SIZE * ATTN_HEADS, ATTN_HEAD_DIM)
    states = (
        state_ssm.reshape(N_EVEN, nb_s, D_SSD, SSD_STATE),
        jnp.pad(state_ssd_conv, ((0, 0), (0, 0), (SC_ROWS - (SSD_CONV - 1), 0), (0, 0))),
        jnp.pad(state_dwconv, ((0, 0), (0, 0), (DW_ROWS - (CONF_KERNEL - 1), 0), (0, 0))),
    )
    m_s = nb_s * seq_s
    out_s = _trunk(x_sample.reshape(m_s, d), nb_s, seq_s, mods_s, W, (m_s, m_s, m_s, m_s),
                   cache=(ck, cv, page_table.reshape(-1)), states=states)

    y_p, k_p, v_p, ssm_p, sc_p, dw_p = out_p
    y_s, k_s, v_s, ssm_s, sc_s, dw_s = out_s
    return (y_p, y_s, k_p, v_p, k_s, v_s, ssm_p, ssm_s, sc_p, sc_s, dw_p, dw_s)
```

```python
import functools

import jax
import jax.numpy as jnp
from jax import lax
from jax.experimental import pallas as pl
from jax.experimental.pallas import tpu as pltpu

F32 = jnp.float32
BF16 = jnp.bfloat16

D_MODEL = 2048
DEPTH = 4
N_EVEN = (DEPTH + 1) // 2
N_ODD = DEPTH // 2
N_SUB = 3
MACARON = 0.5
D_FF = 5632
EPS = 1e-6
D_SSD = 2048
SSD_HEAD_DIM = 64
SSD_HEADS = D_SSD // SSD_HEAD_DIM
SSD_GROUPS = 4
SSD_STATE = 128
SSD_CONV = 4
SSD_CHUNK = 128
SSD_BC = SSD_GROUPS * SSD_STATE
SSD_CONV_DIM = D_SSD + 2 * SSD_BC
D_CONF = 2048
CONF_KERNEL = 31
ATTN_HEADS = 16
ATTN_HEAD_DIM = D_MODEL // ATTN_HEADS
MOBA_BLOCK = 256
MOBA_TOPK = 3
PAGE_SIZE = 128

LANES = 128
SUBLANES = 8
VMEM_LIMIT = 56 << 20
IN_Z, IN_A, IN_G, IN_XBC, IN_DT = 0, 2048, 4096, 6144, 9216
IN_PAD = 9728
DW_ROWS = 32
SC_ROWS = 8
NEG_INF = float("-inf")


def _cparams(sem):
    return pltpu.CompilerParams(dimension_semantics=sem, vmem_limit_bytes=VMEM_LIMIT)


def _dot(a, b):
    return jnp.dot(a, b, preferred_element_type=F32)


def _dot_nt(a, b):
    return lax.dot_general(a, b, (((1,), (1,)), ((), ())), preferred_element_type=F32)


def _silu(x):
    return x * jax.nn.sigmoid(x)


def _softplus(x):
    return jnp.maximum(x, 0.0) + jnp.log1p(jnp.exp(-jnp.abs(x)))


def _adaln(x, g, shift, scale):
    ms = jnp.mean(x * x, axis=-1, keepdims=True)
    y = x * lax.rsqrt(ms + EPS) * g
    return y * (1.0 + scale) + shift


def _split_bf16(x):
    hi = x.astype(BF16)
    lo = (x - hi.astype(F32)).astype(BF16)
    return hi, lo


def _ada_kernel(c_ref, w_ref, b_ref, o_ref):
    cs = _silu(c_ref[...]).astype(BF16)
    o_ref[...] = _dot(cs, w_ref[...].astype(BF16)) + b_ref[...]


def _ada(c_all, w_ada, b_ada):
    rows = c_all.shape[0]
    n = w_ada.shape[-1]
    tn = 1024
    return pl.pallas_call(
        _ada_kernel,
        grid=(DEPTH, n // tn),
        in_specs=[
            pl.BlockSpec((rows, D_MODEL), lambda l, j: (0, 0)),
            pl.BlockSpec((None, D_MODEL, tn), lambda l, j: (l, 0, j)),
            pl.BlockSpec((None, 1, tn), lambda l, j: (l, 0, j)),
        ],
        out_specs=pl.BlockSpec((None, rows, tn), lambda l, j: (l, 0, j)),
        out_shape=jax.ShapeDtypeStruct((DEPTH, rows, n), F32),
        compiler_params=_cparams(("parallel", "parallel")),
        name="ada_mod",
    )(c_all, w_ada, b_ada.reshape(DEPTH, 1, n))


class _Mod:
    def __init__(self, arr, rows_per_seq, per_row):
        self.arr = arr
        self.rows_per_seq = rows_per_seq
        self.per_row = per_row

    def spec(self, tm, width, tiled):
        if self.per_row:
            return pl.BlockSpec((3, tm, width), lambda m, n: (0, m, n if tiled else 0))
        bps = self.rows_per_seq // tm
        return pl.BlockSpec((None, 3, 1, width), lambda m, n: (m // bps, 0, 0, n if tiled else 0))


def _ln_mm_kernel(x_ref, g_ref, mod_ref, w_ref, *rest, n_out, tiles_per_out, n_alias):
    o_refs, h_ref = rest[n_alias:n_alias + n_out], rest[n_alias + n_out]
    j = pl.program_id(1)

    @pl.when(j == 0)
    def _():
        h_ref[...] = _adaln(x_ref[...], g_ref[...], mod_ref[0], mod_ref[1]).astype(BF16)

    if n_out == 1:
        o_refs[0][...] = _dot(h_ref[...], w_ref[...])
    for i in range(n_out if n_out > 1 else 0):
        @pl.when(j // tiles_per_out == i)
        def _(i=i):
            o_refs[i][...] = _dot(h_ref[...], w_ref[...])


def _ln_mm(x, g_norm3, gi, mod, w, wi, tm, tn, n_out=1, slab=None, stacked_prev=()):
    m_rows = x.shape[0]
    n = w.shape[-1]
    tpo = n // n_out // tn
    col = lambda j, i: jnp.clip(j - i * tpo, 0, tpo - 1)
    out_specs = [pl.BlockSpec((tm, tn), lambda m, j: (m, col(j, 0)))]
    out_shape = [jax.ShapeDtypeStruct((m_rows, n // n_out), F32)]
    for i in range(1, n_out):
        if slab is None:
            out_specs.append(pl.BlockSpec((tm, tn), lambda m, j, i=i: (m, col(j, i))))
            out_shape.append(jax.ShapeDtypeStruct((m_rows, n // n_out), F32))
        else:
            out_specs.append(pl.BlockSpec((None, tm, tn), lambda m, j, i=i: (slab[0], m, col(j, i))))
            out_shape.append(jax.ShapeDtypeStruct((slab[1], m_rows, n // n_out), F32))
    n_alias = len(stacked_prev)
    outs = pl.pallas_call(
        functools.partial(_ln_mm_kernel, n_out=n_out, tiles_per_out=tpo, n_alias=n_alias),
        grid=(m_rows // tm, n // tn),
        in_specs=[
            pl.BlockSpec((tm, D_MODEL), lambda m, j: (m, 0)),
            pl.BlockSpec((None, 1, D_MODEL), lambda m, j: (gi, 0, 0)),
            mod.spec(tm, D_MODEL, False),
            pl.BlockSpec((None, D_MODEL, tn), lambda m, j: (wi, 0, j)),
        ] + [pl.BlockSpec(memory_space=pl.ANY)] * n_alias,
        out_specs=out_specs,
        out_shape=out_shape,
        input_output_aliases={4 + i: 1 + i for i in range(n_alias)},
        scratch_shapes=[pltpu.VMEM((tm, D_MODEL), BF16)],
        compiler_params=_cparams(("parallel", "arbitrary")),
        name="ln_proj",
    )(x, g_norm3, mod.arr, w, *stacked_prev)
    return outs[0] if n_out == 1 else outs


def _ffn_kernel(x_ref, g_ref, mod_ref, wg_ref, wu_ref, wd_ref, gf_ref, o_ref, h_ref, *, nf, final_norm):
    f = pl.program_id(1)

    @pl.when(f == 0)
    def _():
        h_ref[...] = _adaln(x_ref[...], g_ref[...], mod_ref[0], mod_ref[1]).astype(BF16)
        o_ref[...] = jnp.zeros_like(o_ref)

    h = h_ref[...]
    a = (_silu(_dot(h, wg_ref[...])) * _dot(h, wu_ref[...])).astype(BF16)
    o_ref[...] += _dot(a, wd_ref[...])

    @pl.when(f == nf - 1)
    def _():
        y = x_ref[...] + (MACARON * (1.0 + mod_ref[2])) * o_ref[...]
        if final_norm:
            ms = jnp.mean(y * y, axis=-1, keepdims=True)
            y = y * lax.rsqrt(ms + EPS) * gf_ref[...]
        o_ref[...] = y


def _ffn(x, g_norm3, gi, mod, wg, wu, wd, l, s, tm, tf, g_final, final_norm=False):
    m_rows = x.shape[0]
    nf = D_FF // tf
    return pl.pallas_call(
        functools.partial(_ffn_kernel, nf=nf, final_norm=final_norm),
        grid=(m_rows // tm, nf),
        in_specs=[
            pl.BlockSpec((tm, D_MODEL), lambda m, f: (m, 0)),
            pl.BlockSpec((None, 1, D_MODEL), lambda m, f: (gi, 0, 0)),
            mod.spec(tm, D_MODEL, False),
            pl.BlockSpec((None, None, D_MODEL, tf), lambda m, f: (l, s, 0, f)),
            pl.BlockSpec((None, None, D_MODEL, tf), lambda m, f: (l, s, 0, f)),
            pl.BlockSpec((None, None, tf, D_MODEL), lambda m, f: (l, s, f, 0)),
            pl.BlockSpec((1, D_MODEL), lambda m, f: (0, 0)),
        ],
        out_specs=pl.BlockSpec((tm, D_MODEL), lambda m, f: (m, 0)),
        out_shape=jax.ShapeDtypeStruct((m_rows, D_MODEL), F32),
        scratch_shapes=[pltpu.VMEM((tm, D_MODEL), BF16)],
        compiler_params=_cparams(("parallel", "arbitrary")),
        name="ffn",
    )(x, g_norm3, mod.arr, wg, wu, wd, g_final)


def _mm_res_kernel(*refs, na):
    a_refs = refs[:na]
    w_refs = refs[na:2 * na]
    x_ref, mod_ref, o_ref = refs[2 * na:]
    acc = _dot(a_refs[0][...].astype(BF16), w_refs[0][...])
    for i in range(1, na):
        acc = acc + _dot(a_refs[i][...].astype(BF16), w_refs[i][...])
    o_ref[...] = x_ref[...] + (1.0 + mod_ref[2]) * acc


def _mm_res(a_list, w, wi, x, mod, tm, tn):
    m_rows = x.shape[0]
    na = len(a_list)
    ka = a_list[0].shape[1]
    in_specs = [pl.BlockSpec((tm, ka), lambda m, j: (m, 0)) for _ in a_list]
    in_specs += [pl.BlockSpec((None, None, ka, tn), lambda m, j, i=i: (wi, i, 0, j)) for i in range(na)]
    in_specs += [pl.BlockSpec((tm, tn), lambda m, j: (m, j)), mod.spec(tm, tn, True)]
    return pl.pallas_call(
        functools.partial(_mm_res_kernel, na=na),
        grid=(m_rows // tm, D_MODEL // tn),
        in_specs=in_specs,
        out_specs=pl.BlockSpec((tm, tn), lambda m, j: (m, j)),
        out_shape=jax.ShapeDtypeStruct((m_rows, D_MODEL), F32),
        compiler_params=_cparams(("parallel", "arbitrary")),
        name="mix_out",
    )(*a_list, *([w] * na), x, mod.arr)


def _cumsum_rows(v):
    rows = v.shape[0]
    row = lax.broadcasted_iota(jnp.int32, v.shape, 0)
    s = 1
    while s < rows:
        v = v + jnp.where(row >= s, pltpu.roll(v, s, axis=0), 0.0)
        s *= 2
    return v


def _expand_heads(v, e, passes):
    out = None
    r = v
    for _ in range(passes):
        hi = r.astype(BF16)
        t = _dot(hi, e)
        out = t if out is None else out + t
        r = r - hi.astype(F32)
    return out


def _ssd_kernel(z_ref, xbc_ref, dt_ref, wc_ref, bc_ref, dtb_ref, alog_ref, dsk_ref, gs_ref, e_ref,
                h0_ref, sc0_ref, y_ref, hn_ref, scn_ref, ht_ref, ext_ref, yacc_ref, *, nc, lv_last):
    q = SSD_CHUNK
    c = pl.program_id(1)

    @pl.when(c == 0)
    def _():
        ht_ref[...] = h0_ref[...].T
        ext_ref[0:SC_ROWS, :] = sc0_ref[...]

    ext_ref[SC_ROWS:SC_ROWS + q, :] = xbc_ref[...]
    base = SC_ROWS - (SSD_CONV - 1)
    acc = bc_ref[...] + wc_ref[0:1, :] * ext_ref[base:base + q, :]
    for k in range(1, SSD_CONV):
        acc = acc + wc_ref[k:k + 1, :] * ext_ref[base + k:base + k + q, :]
    xbc = _silu(acc)
    xs = xbc[:, :D_SSD]
    bm = xbc[:, D_SSD:D_SSD + SSD_BC]
    cm = xbc[:, D_SSD + SSD_BC:]

    dt = _softplus(dt_ref[...] + dtb_ref[...])
    if lv_last < q:
        row = lax.broadcasted_iota(jnp.int32, (q, LANES), 0)
        dt = jnp.where(row < jnp.where(c == nc - 1, lv_last, q), dt, 0.0)
    a_neg = -jnp.exp(alog_ref[...])
    a_cum = _cumsum_rows(dt * a_neg)
    a_cum_t = a_cum.T
    a_last = a_cum[q - 1:q, :]
    e = e_ref[...]
    xdt = xs * _expand_heads(dt, e, 2)
    xw = (xdt * _expand_heads(jnp.exp(a_last - a_cum), e, 2)).astype(BF16)
    expa_e = _expand_heads(jnp.exp(a_cum), e, 2)
    cd_e = _expand_heads(jnp.broadcast_to(jnp.exp(a_last), (SUBLANES, LANES)), e, 3)[0:1, :]
    qi = lax.broadcasted_iota(jnp.int32, (q, q), 0)
    si = lax.broadcasted_iota(jnp.int32, (q, q), 1)
    causal = qi >= si
    lane = lax.broadcasted_iota(jnp.int32, (q, LANES), 1)
    gw = D_SSD // SSD_GROUPS
    for g in range(SSD_GROUPS):
        bg = bm[:, g * SSD_STATE:(g + 1) * SSD_STATE]
        cg = cm[:, g * SSD_STATE:(g + 1) * SSD_STATE].astype(BF16)
        cb = _dot_nt(cg, bg.astype(BF16))
        gsl = slice(g * gw, (g + 1) * gw)
        htg = ht_ref[:, gsl]
        y_inter = _dot(cg, htg.astype(BF16)) * expa_e[:, gsl]
        ht_ref[:, gsl] = cd_e[:, gsl] * htg + _dot(bg.T.astype(BF16), xw[:, gsl])
        for pr in range(gw // LANES):
            lo = g * gw + pr * LANES
            xp = xdt[:, lo:lo + LANES]
            ypair = y_inter[:, pr * LANES:(pr + 1) * LANES]
            for half in range(2):
                h = lo // SSD_HEAD_DIM + half
                seg = a_cum[:, h:h + 1] - a_cum_t[h:h + 1, :]
                dec = jnp.exp(jnp.where(causal, seg, NEG_INF))
                mh = (cb * dec).astype(BF16)
                in_half = (lane >= SSD_HEAD_DIM) if half else (lane < SSD_HEAD_DIM)
                ypair = ypair + _dot(mh, jnp.where(in_half, xp, 0.0).astype(BF16))
            yacc_ref[:, lo:lo + LANES] = ypair
    y = (yacc_ref[...] + dsk_ref[...] * xs) * _silu(z_ref[...])
    for g in range(SSD_GROUPS):
        gsl = slice(g * gw, (g + 1) * gw)
        yg = y[:, gsl]
        ms = jnp.mean(yg * yg, axis=-1, keepdims=True)
        y_ref[:, gsl] = (yg * lax.rsqrt(ms + EPS) * gs_ref[:, gsl]).astype(y_ref.dtype)

    @pl.when(c == nc - 1)
    def _():
        hn_ref[...] = ht_ref[...].T
        scn_ref[...] = ext_ref[lv_last:lv_last + SC_ROWS, :]

    @pl.when(c < nc - 1)
    def _():
        ext_ref[0:SC_ROWS, :] = ext_ref[q:q + SC_ROWS, :]


def _ssd(proj, nb, nc, lv_last, e, p, e_mat, h0, h0_e, sc0, sc0_e):
    q = SSD_CHUNK
    rows = proj.shape[0]
    vec = lambda width: pl.BlockSpec((None, 1, width), lambda b, c: (e, 0, 0))
    return pl.pallas_call(
        functools.partial(_ssd_kernel, nc=nc, lv_last=lv_last),
        grid=(nb, nc),
        in_specs=[
            pl.BlockSpec((q, D_SSD), lambda b, c: (b * nc + c, IN_Z // D_SSD)),
            pl.BlockSpec((q, SSD_CONV_DIM), lambda b, c: (b * nc + c, IN_XBC // SSD_CONV_DIM)),
            pl.BlockSpec((q, LANES), lambda b, c: (b * nc + c, IN_DT // LANES)),
            pl.BlockSpec((None, SSD_CONV, SSD_CONV_DIM), lambda b, c: (e, 0, 0)),
            vec(SSD_CONV_DIM), vec(LANES), vec(LANES), vec(D_SSD), vec(D_SSD),
            pl.BlockSpec((LANES, D_SSD), lambda b, c: (0, 0)),
            pl.BlockSpec((None, None, D_SSD, SSD_STATE), lambda b, c: (h0_e, b, 0, 0)),
            pl.BlockSpec((None, None, SC_ROWS, SSD_CONV_DIM), lambda b, c: (sc0_e, b, 0, 0)),
        ],
        out_specs=[
            pl.BlockSpec((q, D_SSD), lambda b, c: (b * nc + c, 0)),
            pl.BlockSpec((None, D_SSD, SSD_STATE), lambda b, c: (b, 0, 0)),
            pl.BlockSpec((None, SC_ROWS, SSD_CONV_DIM), lambda b, c: (b, 0, 0)),
        ],
        out_shape=[
            jax.ShapeDtypeStruct((rows, D_SSD), BF16),
            jax.ShapeDtypeStruct((nb, D_SSD, SSD_STATE), F32),
            jax.ShapeDtypeStruct((nb, SC_ROWS, SSD_CONV_DIM), F32),
        ],
        scratch_shapes=[
            pltpu.VMEM((SSD_STATE, D_SSD), F32),
            pltpu.VMEM((SC_ROWS + q, SSD_CONV_DIM), F32),
            pltpu.VMEM((q, D_SSD), F32),
        ],
        compiler_params=_cparams(("parallel", "arbitrary")),
        name="ssd_scan",
    )(proj, proj, proj, p["w_conv"], p["b_conv"], p["dt_bias"], p["a_log"], p["d_skip"], p["g_ssd"],
      e_mat, h0, sc0)


def _conf_kernel(a_ref, gt_ref, wdw_ref, bdw_ref, lng_ref, lnb_ref, dw0_ref, u_ref, dwn_ref,
                 ext_ref, cv_ref, sh_ref, *, nc, lv_last):
    q = SSD_CHUNK
    c = pl.program_id(1)

    @pl.when(c == 0)
    def _():
        ext_ref[0:DW_ROWS, :] = dw0_ref[...]

    ext_ref[DW_ROWS:DW_ROWS + q, :] = a_ref[...] * jax.nn.sigmoid(gt_ref[...])
    base = DW_ROWS - (CONF_KERNEL - 1)
    sh_rows = sh_ref.shape[1]
    for r in range(1, SUBLANES):
        sh_ref[r - 1] = ext_ref[r:r + sh_rows, :]
    ct = 256
    for t in range(D_CONF // ct):
        cs = slice(t * ct, (t + 1) * ct)
        acc = bdw_ref[:, cs]
        for k in range(CONF_KERNEL):
            r, al = (base + k) % SUBLANES, (base + k) // SUBLANES * SUBLANES
            win = ext_ref[al:al + q, cs] if r == 0 else sh_ref[r - 1, al:al + q, cs]
            acc = acc + wdw_ref[k:k + 1, cs] * win
        cv_ref[:, cs] = acc
    v = cv_ref[...]
    mu = jnp.mean(v, axis=-1, keepdims=True)
    xc = v - mu
    var = jnp.mean(xc * xc, axis=-1, keepdims=True)
    yn = xc * lax.rsqrt(var + EPS) * lng_ref[...] + lnb_ref[...]
    u_ref[...] = _silu(yn).astype(u_ref.dtype)

    @pl.when(c == nc - 1)
    def _():
        dwn_ref[...] = ext_ref[lv_last:lv_last + DW_ROWS, :]

    @pl.when(c < nc - 1)
    def _():
        ext_ref[0:DW_ROWS, :] = ext_ref[q:q + DW_ROWS, :]


def _conf(proj, nb, nc, lv_last, e, p, dw0, dw0_e):
    q = SSD_CHUNK
    rows = proj.shape[0]
    vec = lambda: pl.BlockSpec((None, 1, D_CONF), lambda b, c: (e, 0, 0))
    return pl.pallas_call(
        functools.partial(_conf_kernel, nc=nc, lv_last=lv_last),
        grid=(nb, nc),
        in_specs=[
            pl.BlockSpec((q, D_CONF), lambda b, c: (b * nc + c, IN_A // D_CONF)),
            pl.BlockSpec((q, D_CONF), lambda b, c: (b * nc + c, IN_G // D_CONF)),
            pl.BlockSpec((None, DW_ROWS, D_CONF), lambda b, c: (e, 0, 0)),
            vec(), vec(), vec(),
            pl.BlockSpec((None, None, DW_ROWS, D_CONF), lambda b, c: (dw0_e, b, 0, 0)),
        ],
        out_specs=[
            pl.BlockSpec((q, D_CONF), lambda b, c: (b * nc + c, 0)),
            pl.BlockSpec((None, DW_ROWS, D_CONF), lambda b, c: (b, 0, 0)),
        ],
        out_shape=[
            jax.ShapeDtypeStruct((rows, D_CONF), BF16),
            jax.ShapeDtypeStruct((nb, DW_ROWS, D_CONF), F32),
        ],
        scratch_shapes=[
            pltpu.VMEM((DW_ROWS + q, D_CONF), F32),
            pltpu.VMEM((q, D_CONF), F32),
            pltpu.VMEM((SUBLANES - 1, DW_ROWS + q - SUBLANES, D_CONF), F32),
        ],
        compiler_params=_cparams(("parallel", "arbitrary")),
        name="conf_conv",
    )(proj, proj, p["w_dw"], p["b_dw"], p["ln_g"], p["ln_b"], dw0)


def _moba_prompt_kernel(q_ref, k_ref, v_ref, o_ref, s_ref, *, seq):
    blk = MOBA_BLOCK
    nb = seq // blk
    scale = ATTN_HEAD_DIM ** -0.5
    q = q_ref[...]
    k = k_ref[...]
    km = jnp.mean(k.reshape(nb, blk, ATTN_HEAD_DIM), axis=1)
    qh, ql = _split_bf16(q)
    kh, kl = _split_bf16(km)
    sc_t = _dot_nt(kh, qh) + _dot_nt(kh, ql) + _dot_nt(kl, qh)
    qs = (q * scale).astype(BF16)
    kb = k.astype(BF16)
    v_t = v_ref[...].T.astype(BF16)
    sub = lax.broadcasted_iota(jnp.int32, (nb, blk), 0)
    kidx = lax.broadcasted_iota(jnp.int32, (blk, blk), 0)
    qidx = lax.broadcasted_iota(jnp.int32, (blk, blk), 1)
    for j in range(nb):
        sc = sc_t[:, j * blk:(j + 1) * blk]
        rank = jnp.zeros((nb, blk), F32)
        for i in range(j):
            r = sc[i:i + 1, :]
            beats = jnp.where(r > sc, 1.0, jnp.where(r == sc, jnp.where(sub > i, 1.0, 0.0), 0.0))
            rank = rank + beats
        bias = jnp.where(rank < MOBA_TOPK, jnp.where(sub < j, 0.0, NEG_INF), NEG_INF)
        qj = qs[j * blk:(j + 1) * blk, :]
        for i in range(j + 1):
            s = _dot_nt(kb[i * blk:(i + 1) * blk, :], qj)
            if i < j:
                s = s + bias[i:i + 1, :]
            else:
                s = jnp.where(kidx <= qidx, s, NEG_INF)
            s_ref[i * blk:(i + 1) * blk, :] = s
        nk = (j + 1) * blk
        s_all = s_ref[0:nk, :]
        m = jnp.max(s_all, axis=0, keepdims=True)
        p = jnp.exp(s_all - m)
        l = jnp.sum(p, axis=0, keepdims=True)
        o_t = _dot(v_t[:, 0:nk], p.astype(BF16)) / l
        o_ref[j * blk:(j + 1) * blk, :] = o_t.T.astype(o_ref.dtype)


def _moba_prompt(q, k, v, o, nb, seq):
    spec = pl.BlockSpec((seq, ATTN_HEAD_DIM), lambda b, h: (b, h))
    kv_spec = pl.BlockSpec((None, seq, ATTN_HEAD_DIM), lambda b, h: (o, b, h))
    return pl.pallas_call(
        functools.partial(_moba_prompt_kernel, seq=seq),
        grid=(nb, ATTN_HEADS),
        in_specs=[spec, kv_spec, kv_spec],
        out_specs=spec,
        out_shape=jax.ShapeDtypeStruct((nb * seq, D_MODEL), BF16),
        scratch_shapes=[pltpu.VMEM((seq, MOBA_BLOCK), F32)],
        compiler_params=_cparams(("parallel", "parallel")),
        name="moba_prompt",
    )(q, k, v)


PAGES_PER_STEP = 8
BLOCKS_PER_STEP = PAGES_PER_STEP * PAGE_SIZE // MOBA_BLOCK
HEAD_HALVES = 2
HEADS_LOW = ATTN_HEADS // HEAD_HALVES
FOLD = HEADS_LOW * ATTN_HEAD_DIM
PKEYS = HEAD_HALVES * MOBA_BLOCK


def _load_page_folded(ref):
    return jnp.concatenate(
        [ref[pl.ds(low, PAGE_SIZE * HEAD_HALVES, stride=HEADS_LOW), :] for low in range(HEADS_LOW)], axis=1)


def _fold_lanes(x, nq):
    rows = x.shape[0]
    first = lax.broadcasted_iota(jnp.int32, (rows, FOLD), 0) < HEADS_LOW * nq
    return jnp.where(first, x[:, :FOLD], x[:, FOLD:])


def _own_half(shape, nq):
    col = lax.broadcasted_iota(jnp.int32, shape, 1)
    row = lax.broadcasted_iota(jnp.int32, shape, 0)
    return (col % HEAD_HALVES) == (row // (HEADS_LOW * nq))


def _block_diag_queries(q8):
    nq = q8.shape[0]
    rows = ATTN_HEADS * nq
    qrep = jnp.concatenate([q8] * ATTN_HEADS, axis=0)
    rh = lax.broadcasted_iota(jnp.int32, (rows, D_MODEL), 0) // nq
    ch = lax.broadcasted_iota(jnp.int32, (rows, D_MODEL), 1) // ATTN_HEAD_DIM
    return jnp.where(rh == ch, qrep, 0.0), rh == ch


def _sample_scores_kernel(pt_ref, q_ref, *refs, nsteps):
    k_refs = refs[:PAGES_PER_STEP]
    s_ref, bs_ref, bm_ref = refs[PAGES_PER_STEP:]
    st = pl.program_id(1)
    scale = ATTN_HEAD_DIM ** -0.5
    nq = q_ref.shape[0]
    qbd, _ = _block_diag_queries(q_ref[...])
    qf = _fold_lanes(qbd, nq).astype(BF16)
    own = _own_half((ATTN_HEADS * nq, PKEYS), nq)

    @pl.when(st == 0)
    def _():
        bs_ref[...] = jnp.zeros_like(bs_ref)
        bm_ref[...] = jnp.full(bm_ref.shape, NEG_INF, F32)

    bs = bs_ref[...]
    bm = bm_ref[...]
    lane = lax.broadcasted_iota(jnp.int32, bs.shape, 1)
    ppb = MOBA_BLOCK // PAGE_SIZE
    for j in range(BLOCKS_PER_STEP):
        kblk = jnp.concatenate([_load_page_folded(k_refs[ppb * j + r]) for r in range(ppb)], axis=0).astype(BF16)
        sj = _dot_nt(qf, kblk) * scale
        sm = jnp.where(own, sj, NEG_INF)
        s_ref[j] = sm
        bmean = jnp.sum(jnp.where(own, sj, 0.0), axis=-1, keepdims=True) * (1.0 / MOBA_BLOCK)
        here = lane == st * BLOCKS_PER_STEP + j
        bs = jnp.where(here, bmean, bs)
        bm = jnp.where(here, jnp.max(sm, axis=-1, keepdims=True), bm)
    bs_ref[...] = bs
    bm_ref[...] = bm


def _top_blocks(bs, nvalid):
    lane = lax.broadcasted_iota(jnp.int32, bs.shape, 1).astype(F32)
    cur = jnp.where(lane < nvalid, bs, NEG_INF)
    sel = jnp.zeros(bs.shape, F32)
    for _ in range(MOBA_TOPK):
        m = jnp.max(cur, axis=-1, keepdims=True)
        idx = jnp.min(jnp.where(cur == m, lane, 1e9), axis=-1, keepdims=True)
        pick = jnp.where(lane == idx, jnp.where(m > NEG_INF, 1.0, 0.0), 0.0)
        sel = jnp.maximum(sel, pick)
        cur = jnp.where(pick > 0.5, NEG_INF, cur)
    return sel


def _sample_attn_kernel(pt_ref, s_ref, bs_ref, bm_ref, q_ref, kn_ref, vn_ref, *refs, nsteps, nblocks):
    v_refs = refs[:PAGES_PER_STEP]
    o_ref, p_ref, acc_ref, l_ref = refs[PAGES_PER_STEP:]
    st = pl.program_id(1)
    nq = q_ref.shape[0]
    rows = ATTN_HEADS * nq
    scale = ATTN_HEAD_DIM ** -0.5

    @pl.when(st == 0)
    def _():
        sel = _top_blocks(bs_ref[...], nblocks)
        lane = lax.broadcasted_iota(jnp.int32, (rows, LANES), 1)
        qbd, _ = _block_diag_queries(q_ref[...])
        pad = jnp.zeros((LANES - nq, D_MODEL), F32)
        kn = jnp.concatenate([kn_ref[...], pad], axis=0).astype(BF16)
        vn = jnp.concatenate([vn_ref[...], pad], axis=0).astype(BF16)
        s_own = _dot_nt(qbd.astype(BF16), kn) * scale
        qpos = lax.broadcasted_iota(jnp.int32, (rows, LANES), 0) % nq
        s_own = jnp.where(lane <= qpos, s_own, NEG_INF)

        def sel_col(n):
            return jnp.max(jnp.where(lane == n, sel, 0.0), axis=-1, keepdims=True) > 0.5

        m_past = jnp.max(jnp.where(sel > 0.5, bm_ref[...], NEG_INF), axis=-1, keepdims=True)
        m = jnp.maximum(m_past, jnp.max(s_own, axis=-1, keepdims=True))
        p_own = jnp.exp(s_own - m)

        def p_body(n, l):
            pn = jnp.where(sel_col(n), jnp.exp(s_ref[n] - m), 0.0)
            p_ref[n] = pn.astype(BF16)
            return l + jnp.sum(pn, axis=-1, keepdims=True)

        l = lax.fori_loop(0, nblocks, p_body, jnp.sum(p_own, axis=-1, keepdims=True), unroll=4)
        l_ref[...] = jnp.broadcast_to(l, l_ref.shape)
        acc_ref[...] = _fold_lanes(_dot(p_own.astype(BF16), vn), nq)

    ppb = MOBA_BLOCK // PAGE_SIZE
    acc = acc_ref[...]
    for j in range(BLOCKS_PER_STEP):
        vblk = jnp.concatenate([_load_page_folded(v_refs[ppb * j + r]) for r in range(ppb)], axis=0).astype(BF16)
        acc = acc + _dot(p_ref[st * BLOCKS_PER_STEP + j], vblk)
    acc_ref[...] = acc

    @pl.when(st == nsteps - 1)
    def _():
        rl = (lax.broadcasted_iota(jnp.int32, (rows, FOLD), 0) // nq) % HEADS_LOW
        cl = lax.broadcasted_iota(jnp.int32, (rows, FOLD), 1) // ATTN_HEAD_DIM
        o_full = jnp.where(rl == cl, acc_ref[...] / l_ref[:, 0:1], 0.0).reshape(ATTN_HEADS, nq, FOLD)
        o_ref[...] = jnp.concatenate(
            [jnp.sum(o_full[half * HEADS_LOW:(half + 1) * HEADS_LOW], axis=0) for half in range(HEAD_HALVES)],
            axis=1)


def _page_specs(o, n_pages_seq):
    specs = []
    for r in range(PAGES_PER_STEP):
        specs.append(pl.BlockSpec(
            (None, None, PAGE_SIZE * ATTN_HEADS, ATTN_HEAD_DIM),
            lambda b, st, pt, r=r: (o, pt[b * n_pages_seq + st * PAGES_PER_STEP + r], 0, 0)))
    return specs


def _moba_sample(q, k_new, v_new, cache_k, cache_v, pt_flat, o, nseq, nq):
    n_pages_seq = pt_flat.shape[0] // nseq
    nblocks = n_pages_seq * PAGE_SIZE // MOBA_BLOCK
    nsteps = n_pages_seq // PAGES_PER_STEP
    rows = ATTN_HEADS * nq
    s_all, bscore, bmax = pl.pallas_call(
        functools.partial(_sample_scores_kernel, nsteps=nsteps),
        grid_spec=pltpu.PrefetchScalarGridSpec(
            num_scalar_prefetch=1,
            grid=(nseq, nsteps),
            in_specs=[pl.BlockSpec((nq, D_MODEL), lambda b, st, pt: (b, 0))] + _page_specs(o, n_pages_seq),
            out_specs=[
                pl.BlockSpec((None, BLOCKS_PER_STEP, rows, PKEYS), lambda b, st, pt: (b, st, 0, 0)),
                pl.BlockSpec((None, rows, LANES), lambda b, st, pt: (b, 0, 0)),
                pl.BlockSpec((None, rows, LANES), lambda b, st, pt: (b, 0, 0)),
            ],
        ),
        out_shape=[
            jax.ShapeDtypeStruct((nseq, nblocks, rows, PKEYS), F32),
            jax.ShapeDtypeStruct((nseq, rows, LANES), F32),
            jax.ShapeDtypeStruct((nseq, rows, LANES), F32),
        ],
        compiler_params=_cparams(("parallel", "arbitrary")),
        name="moba_sample_scores",
    )(pt_flat, q, *([cache_k] * PAGES_PER_STEP))
    return pl.pallas_call(
        functools.partial(_sample_attn_kernel, nsteps=nsteps, nblocks=nblocks),
        grid_spec=pltpu.PrefetchScalarGridSpec(
            num_scalar_prefetch=1,
            grid=(nseq, nsteps),
            in_specs=[
                pl.BlockSpec((None, nblocks, rows, PKEYS), lambda b, st, pt: (b, 0, 0, 0),
                             pipeline_mode=pl.Buffered(1)),
                pl.BlockSpec((None, rows, LANES), lambda b, st, pt: (b, 0, 0)),
                pl.BlockSpec((None, rows, LANES), lambda b, st, pt: (b, 0, 0)),
                pl.BlockSpec((nq, D_MODEL), lambda b, st, pt: (b, 0)),
                pl.BlockSpec((None, nq, D_MODEL), lambda b, st, pt: (o, b, 0)),
                pl.BlockSpec((None, nq, D_MODEL), lambda b, st, pt: (o, b, 0)),
            ] + _page_specs(o, n_pages_seq),
            out_specs=pl.BlockSpec((nq, D_MODEL), lambda b, st, pt: (b, 0)),
            scratch_shapes=[
                pltpu.VMEM((nblocks, rows, PKEYS), BF16),
                pltpu.VMEM((rows, FOLD), F32),
                pltpu.VMEM((rows, LANES), F32),
            ],
        ),
        out_shape=jax.ShapeDtypeStruct((nseq * nq, D_MODEL), F32),
        compiler_params=_cparams(("parallel", "arbitrary")),
        name="moba_sample_attn",
    )(pt_flat, s_all, bscore, bmax, q, k_new, v_new, *([cache_v] * PAGES_PER_STEP))


def _trunk(x, nseq, seq, mods, W, tiles, cache=None, states=None):
    tm_proj, tm_ffn, tm_out, tm_qkv = tiles
    q = SSD_CHUNK
    seq_pad = -(-seq // q) * q
    nc = seq_pad // q
    lv_last = seq - (nc - 1) * q
    ssm, scs, dws = [], [], []
    kv_prev = ()
    ffn =functools.partial(_ffn, wg=W["wg"], wu=W["wu"], wd=W["wd"], tm=tm_ffn, tf=512, g_final=W["g_final"])
    for l in range(DEPTH):
        x = ffn(x, W["g_norm"], l * N_SUB, mods[l][0], l=l, s=0)
        if l % 2 == 0:
            e = l // 2
            proj = _ln_mm(x, W["g_norm"], l * N_SUB + 1, mods[l][1], W["w_in"], e, tm_proj, 512)
            if seq_pad != seq:
                proj = jnp.pad(proj.reshape(nseq, seq, IN_PAD), ((0, 0), (0, seq_pad - seq), (0, 0)))
                proj = proj.reshape(nseq * seq_pad, IN_PAD)
            if states is None:
                h0, h0_e, sc0, sc0_e, dw0, dw0_e = W["zero_h"], 0, W["zero_sc"], 0, W["zero_dw"], 0
            else:
                h0, sc0, dw0 = states
                h0_e = sc0_e = dw0_e = e
            y, hn, scn = _ssd(proj, nseq, nc, lv_last, e, W["ssd"], W["e_mat"], h0, h0_e, sc0, sc0_e)
            u, dwn = _conf(proj, nseq, nc, lv_last, e, W["conf"], dw0, dw0_e)
            if seq_pad != seq:
                y = y.reshape(nseq, seq_pad, D_SSD)[:, :seq].reshape(nseq * seq, D_SSD)
                u = u.reshape(nseq, seq_pad, D_CONF)[:, :seq].reshape(nseq * seq, D_CONF)
            ssm.append(hn.reshape(nseq, SSD_HEADS, SSD_HEAD_DIM, SSD_STATE))
            scs.append(scn[:, SC_ROWS - (SSD_CONV - 1):])
            dws.append(dwn[:, DW_ROWS - (CONF_KERNEL - 1):])
            x = _mm_res([y, u], W["w_out"], e, x, mods[l][1], tm_out, 512)
        else:
            o = l // 2
            qm, k_all, v_all = _ln_mm(x, W["g_norm"], l * N_SUB + 1, mods[l][1], W["w_qkv"], o, tm_qkv, 1024,
                                      n_out=3, slab=(o, N_ODD), stacked_prev=kv_prev)
            kv_prev = (k_all, v_all)
            if cache is None:
                att = _moba_prompt(qm, k_all, v_all, o, nseq, seq)
            else:
                cache_k, cache_v, pt_flat = cache
                att = _moba_sample(qm, k_all, v_all, cache_k, cache_v, pt_flat, o, nseq, seq)
            x = _mm_res([att], W["w_o"], o, x, mods[l][1], tm_out, 512)
        x = ffn(x, W["g_norm"], l * N_SUB + 2, mods[l][2], l=l, s=1, final_norm=(l == DEPTH - 1))
    kv_shape = (N_ODD, nseq, seq, ATTN_HEADS, ATTN_HEAD_DIM)
    return (x.reshape(nseq, seq, D_MODEL), kv_prev[0].reshape(kv_shape), kv_prev[1].reshape(kv_shape),
            jnp.stack(ssm), jnp.stack(scs), jnp.stack(dws))


def kernel(x_prompt, x_sample, cache_k, cache_v, page_table, state_ssm, state_ssd_conv, state_dwconv, c_prompt, c_sample, w_ada, b_ada, g_norm, w_ff_gate, w_ff_up, w_ff_down, w_in_even, w_conv_ssd, b_conv_ssd, dt_bias, a_log, d_skip, g_ssd_norm, w_dw, b_dw, g_conf_ln, b_conf_ln, w_out_even, w_qkv, w_o, g_final):
    nb_p, seq_p, _ = x_prompt.shape
    nb_s, seq_s, _ = x_sample.shape
    d = D_MODEL

    i1, i2, i3 = D_SSD, D_SSD + SSD_CONV_DIM, D_SSD + SSD_CONV_DIM + SSD_HEADS
    w_in = jnp.concatenate([
        w_in_even[..., :i1], w_in_even[..., i3:], w_in_even[..., i1:i2], w_in_even[..., i2:i3],
        jnp.zeros((N_EVEN, d, IN_PAD - IN_DT - SSD_HEADS), w_in_even.dtype)], axis=-1).astype(BF16)
    pad_h = lambda a: jnp.pad(a, ((0, 0), (0, LANES - SSD_HEADS))).reshape(N_EVEN, 1, LANES)
    lane_head = jnp.arange(D_SSD)[None, :] // SSD_HEAD_DIM
    W = {
        "g_norm": g_norm.reshape(DEPTH * N_SUB, 1, d),
        "wg": w_ff_gate.astype(BF16), "wu": w_ff_up.astype(BF16), "wd": w_ff_down.astype(BF16),
        "w_in": w_in,
        "w_out": w_out_even.astype(BF16).reshape(N_EVEN, 2, D_SSD, d),
        "w_qkv": w_qkv.astype(BF16),
        "w_o": w_o.astype(BF16).reshape(N_ODD, 1, d, d),
        "g_final": g_final.reshape(1, d),
        "e_mat": (jnp.arange(LANES)[:, None] == lane_head).astype(BF16),
        "ssd": {
            "w_conv": w_conv_ssd, "b_conv": b_conv_ssd.reshape(N_EVEN, 1, SSD_CONV_DIM),
            "dt_bias": pad_h(dt_bias), "a_log": pad_h(a_log),
            "d_skip": jnp.repeat(d_skip, SSD_HEAD_DIM, axis=-1).reshape(N_EVEN, 1, D_SSD),
            "g_ssd": g_ssd_norm.reshape(N_EVEN, 1, D_SSD),
        },
        "conf": {
            "w_dw": jnp.pad(w_dw, ((0, 0), (0, DW_ROWS - CONF_KERNEL), (0, 0))),
            "b_dw": b_dw.reshape(N_EVEN, 1, D_CONF),
            "ln_g": g_conf_ln.reshape(N_EVEN, 1, D_CONF), "ln_b": b_conf_ln.reshape(N_EVEN, 1, D_CONF),
        },
        "zero_h": jnp.zeros((1, nb_p, D_SSD, SSD_STATE), F32),
        "zero_sc": jnp.zeros((1, nb_p, SC_ROWS, SSD_CONV_DIM), F32),
        "zero_dw": jnp.zeros((1, nb_p, DW_ROWS, D_CONF), F32),
    }

    n_c = nb_p + nb_s
    c_rows = -(-n_c // SUBLANES) * SUBLANES
    c_all = jnp.pad(jnp.concatenate([c_prompt, c_sample], axis=0), ((0, c_rows - n_c), (0, 0)))
    mod_all = _ada(c_all, w_ada, b_ada).reshape(DEPTH, c_rows, N_SUB, 3, d)
    mods_p, mods_s = [], []
    for l in range(DEPTH):
        mp, ms = [], []
        for s in range(N_SUB):
            mp.append(_Mod(mod_all[l, :nb_p, s][:, :, None, :], seq_p, False))
            rows = jnp.repeat(mod_all[l, nb_p:n_c, s], seq_s, axis=0)
            ms.append(_Mod(jnp.transpose(rows, (1, 0, 2)), seq_s, True))
        mods_p.append(mp)
        mods_s.append(ms)

    out_p = _trunk(x_prompt.reshape(nb_p * seq_p, d), nb_p, seq_p, mods_p, W, (1024, 512, 1024, 512))

    n_phys = cache_k.shape[1]
    ck = cache_k.reshape(N_ODD, n_phys, PAGE_SIZE * ATTN_HEADS, ATTN_HEAD_DIM)
    cv = cache_v.reshape(N_ODD, n_phys, PAGE_SIZE * ATTN_HEADS, ATTN_HEAD_DIM)
    states = (
        state_ssm.reshape(N_EVEN, nb_s, D_SSD, SSD_STATE),
        jnp.pad(state_ssd_conv, ((0, 0), (0, 0), (SC_ROWS - (SSD_CONV - 1), 0), (0, 0))),
        jnp.pad(state_dwconv, ((0, 0), (0, 0), (DW_ROWS - (CONF_KERNEL - 1), 0), (0, 0))),
    )
    m_s = nb_s * seq_s
    out_s = _trunk(x_sample.reshape(m_s, d), nb_s, seq_s, mods_s, W, (m_s, m_s, m_s, m_s),
                   cache=(ck, cv, page_table.reshape(-1)), states=states)

    y_p, k_p, v_p, ssm_p, sc_p, dw_p = out_p
    y_s, k_s, v_s, ssm_s, sc_s, dw_s = out_s
    return (y_p, y_s, k_p, v_p, k_s, v_s, ssm_p, ssm_s, sc_p, sc_s, dw_p, dw_s)
```
